```python
import jax, jax.numpy as jnp
from jax import lax
import numpy as np

D_MODEL = 1024
BATCH = 4
SEQ = 8192
DEPTH = 1

N_META = 16
GLA_HEADS = 4
GLA_DK = D_MODEL // 2
GLA_DV = D_MODEL
GLA_DKH = GLA_DK // GLA_HEADS
GLA_DVH = GLA_DV // GLA_HEADS
GLA_RANK = 16
GATE_TAU = 16.0
CHUNK = 64
META_PAD = CHUNK - N_META
CONF_CH = D_MODEL
CONF_K = 31
D_FF = 2816
FFN_K = 3
IN_WIDTHS = (GLA_DK, GLA_DK, GLA_DV, GLA_DV, GLA_RANK, 2 * CONF_CH, D_MODEL, D_MODEL)
N_IN = sum(IN_WIDTHS)
RMS_EPS = 1e-6
LN_EPS = 1e-5

kernel_name = "gla_conformer_gated_hybrid_block"


def split_points(widths):
    pts, acc = [], 0
    for w in widths[:-1]:
        acc += w
        pts.append(acc)
    return pts


def rms_norm(x, g):
    xf = x.astype(jnp.float32)
    y = xf * lax.rsqrt(jnp.mean(xf * xf, axis=-1, keepdims=True) + RMS_EPS)
    return (y * g.astype(jnp.float32)).astype(x.dtype)


def layer_norm(x, g, b):
    xf = x.astype(jnp.float32)
    mu = jnp.mean(xf, axis=-1, keepdims=True)
    var = jnp.mean(jnp.square(xf - mu), axis=-1, keepdims=True)
    y = (xf - mu) * lax.rsqrt(var + LN_EPS)
    return (y * g.astype(jnp.float32) + b.astype(jnp.float32)).astype(x.dtype)


def causal_dwconv(x, w, b):
    K, C = w.shape
    out = lax.conv_general_dilated(
        x, w[:, None, :].astype(x.dtype), window_strides=(1,), padding=[(K - 1, 0)],
        dimension_numbers=("NWC", "WIO", "NWC"), feature_group_count=C)
    return out + b.astype(x.dtype)


def gla_chunked(q, k, v, log_a):
    out_dtype = v.dtype
    q, k, v, log_a = (t.astype(jnp.float32) for t in (q, k, v, log_a))
    B, H, T, dk = q.shape
    dv = v.shape[-1]
    n = T // CHUNK

    def to_chunks(t):
        return jnp.moveaxis(t.reshape(B, H, n, CHUNK, t.shape[-1]), 2, 0)

    qc, kc, vc, ac = (to_chunks(t) for t in (q, k, v, log_a))
    mask = jnp.tril(jnp.ones((CHUNK, CHUNK), dtype=bool))[:, :, None]

    def step(S, inp):
        qi, ki, vi, ai = inp
        b = jnp.cumsum(ai, axis=-2)
        b_last = b[..., -1:, :]
        o_inter = jnp.einsum('bhck,bhkv->bhcv', qi * jnp.exp(b), S)
        diff = b[..., :, None, :] - b[..., None, :, :]
        decay = jnp.where(mask, jnp.exp(jnp.where(mask, diff, 0.0)), 0.0)
        scores = jnp.einsum('bhik,bhjk,bhijk->bhij', qi, ki, decay)
        o = o_inter + jnp.einsum('bhij,bhjv->bhiv', scores, vi)
        S_new = S * jnp.exp(b_last)[..., 0, :, None] + jnp.einsum(
            'bhck,bhcv->bhkv', ki * jnp.exp(b_last - b), vi)
        return S_new, o

    S0 = jnp.zeros((B, H, dk, dv), jnp.float32)
    _, oc = lax.scan(step, S0, (qc, kc, vc, ac))
    return jnp.moveaxis(oc, 0, 2).reshape(B, H, T, dv).astype(out_dtype)


def setup_inputs(seed: int = 0) -> dict:
    key = jax.random.key(seed)
    ks = jax.random.split(key, 24)
    nrm = lambda k, shape, s: jax.random.normal(k, shape, jnp.float32) * s
    return {
        "x": nrm(ks[0], (BATCH, SEQ, D_MODEL), 1.0),
        "meta_tokens": nrm(ks[1], (N_META, D_MODEL), 1.0),
        "norm_mix_g": 1.0 + nrm(ks[2], (DEPTH, D_MODEL), 0.02),
        "w_in": nrm(ks[3], (DEPTH, D_MODEL, N_IN), D_MODEL ** -0.5),
        "w_alpha_up": nrm(ks[4], (DEPTH, GLA_RANK, GLA_DK), GLA_RANK ** -0.5),
        "b_alpha": nrm(ks[5], (DEPTH, GLA_DK), 0.1),
        "gla_norm_g": 1.0 + nrm(ks[6], (DEPTH, GLA_DV), 0.02),
        "w_gla_o": nrm(ks[7], (DEPTH, GLA_DV, D_MODEL), GLA_DV ** -0.5),
        "conf_dw_w": nrm(ks[8], (DEPTH, CONF_K, CONF_CH), CONF_K ** -0.5),
        "conf_dw_b": nrm(ks[9], (DEPTH, CONF_CH), 0.02),
        "conf_ln_g": 1.0 + nrm(ks[10], (DEPTH, CONF_CH), 0.02),
        "conf_ln_b": nrm(ks[11], (DEPTH, CONF_CH), 0.02),
        "w_conf_o": nrm(ks[12], (DEPTH, CONF_CH, D_MODEL), CONF_CH ** -0.5),
        "w_out": nrm(ks[13], (DEPTH, D_MODEL, D_MODEL), D_MODEL ** -0.5),
        "norm_ffn_g": 1.0 + nrm(ks[14], (DEPTH, D_MODEL), 0.02),
        "w_up": nrm(ks[15], (DEPTH, D_MODEL, 2 * D_FF), D_MODEL ** -0.5),
        "ffn_dw_w": nrm(ks[16], (DEPTH, FFN_K, D_FF), FFN_K ** -0.5),
        "ffn_dw_b": nrm(ks[17], (DEPTH, D_FF), 0.02),
        "w_down": nrm(ks[18], (DEPTH, D_FF, D_MODEL), D_FF ** -0.5),
        "final_norm_g": 1.0 + nrm(ks[19], (D_MODEL,), 0.02),
    }


def reference(x, meta_tokens, norm_mix_g, w_in, w_alpha_up, b_alpha, gla_norm_g, w_gla_o,
              conf_dw_w, conf_dw_b, conf_ln_g, conf_ln_b, w_conf_o, w_out, norm_ffn_g,
              w_up, ffn_dw_w, ffn_dw_b, w_down, final_norm_g):
    B, S, D = x.shape
    L = S + N_META
    meta = jnp.broadcast_to(meta_tokens[None].astype(x.dtype), (B, N_META, D))
    h = jnp.concatenate([meta, x], axis=1)
    pts = split_points(IN_WIDTHS)
    seq_pad = [(0, 0), (0, 0), (META_PAD, 0), (0, 0)]

    def heads(t):
        return t.reshape(B, L, GLA_HEADS, -1).transpose(0, 2, 1, 3)

    for l in range(DEPTH):
        u = rms_norm(h, norm_mix_g[l])
        proj = u @ w_in[l]
        q, k, v, r, a_lr, c_in, g_gla, g_conf = jnp.split(proj, pts, axis=-1)

        log_a = jax.nn.log_sigmoid(
            (a_lr @ w_alpha_up[l] + b_alpha[l]).astype(jnp.float32)) / GATE_TAU
        qh = jnp.pad(heads(q) * (GLA_DKH ** -0.5), seq_pad)
        kh = jnp.pad(heads(k), seq_pad)
        vh = jnp.pad(heads(v), seq_pad)
        ah = jnp.pad(heads(log_a), seq_pad)
        o = gla_chunked(qh, kh, vh, ah)[:, :, META_PAD:]
        o = rms_norm(o.transpose(0, 2, 1, 3), gla_norm_g[l].reshape(GLA_HEADS, GLA_DVH))
        o = o.reshape(B, L, GLA_DV) * jax.nn.silu(r)
        br_gla = o @ w_gla_o[l]

        c1, c2 = jnp.split(c_in, 2, axis=-1)
        c = c1 * jax.nn.sigmoid(c2)
        c = causal_dwconv(c, conf_dw_w[l], conf_dw_b[l])
        c = layer_norm(c, conf_ln_g[l], conf_ln_b[l])
        br_conf = jax.nn.silu(c) @ w_conf_o[l]

        merged = jax.nn.sigmoid(g_gla) * br_gla + jax.nn.sigmoid(g_conf) * br_conf
        h = h + merged @ w_out[l]

        u = rms_norm(h, norm_ffn_g[l])
        a, bv = jnp.split(u @ w_up[l], 2, axis=-1)
        a = causal_dwconv(a, ffn_dw_w[l], ffn_dw_b[l])
        h = h + (jax.nn.silu(a) * bv) @ w_down[l]

    h = rms_norm(h, final_norm_g)
    return h[:, N_META:]
```

```python
import functools

import jax
import jax.numpy as jnp
from jax import lax
from jax.experimental import pallas as pl
from jax.experimental.pallas import tpu as pltpu

F32 = jnp.float32
BF16 = jnp.bfloat16

D_MODEL = 1024
N_META = 16
GLA_HEADS = 4
GLA_DK = 512
GLA_DV = 1024
GLA_DKH = GLA_DK // GLA_HEADS
GLA_DVH = GLA_DV // GLA_HEADS
GLA_RANK = 16
GATE_TAU = 16.0
CHUNK = 64
SUB = 16
META_PAD = CHUNK - N_META
CONF_K = 31
CONV_HIST = 32
D_FF = 2816
FFN_K = 3
FFN_HIST = 8
RMS_EPS = 1e-6
LN_EPS = 1e-5
LANES = 128
RANK_PAD = LANES
VMEM_LIMIT = 56 * 1024 * 1024

_COLS = dict(q=(0, 512), k=(512, 1024), v=(1024, 2048), r=(2048, 3072),
             c1=(3072, 4096), c2=(4096, 5120), gg=(5120, 6144), gc=(6144, 7168))
N_MAIN = 7168


def _sigmoid(x):
    return 1.0 / (1.0 + jnp.exp(-x))


def _log_sigmoid(x):
    return jnp.minimum(x, 0.0) - jnp.log(1.0 + jnp.exp(-jnp.abs(x)))


def _const_spec(shape):
    nd = len(shape)
    return pl.BlockSpec(shape, lambda *_: (0,) * nd, pipeline_mode=pl.Buffered(1))


def _params(sem):
    return pltpu.CompilerParams(dimension_semantics=sem, vmem_limit_bytes=VMEM_LIMIT)


def _proj_kernel(x_ref, g_ref, w_ref, walr_ref, wau_ref, bal_ref,
                 q_ref, k_ref, v_ref, rs_ref, la_ref, c_ref, sgg_ref, sgc_ref):
    x = x_ref[0]
    ms = jnp.mean(x * x, axis=-1, keepdims=True)
    u = (x * lax.rsqrt(ms + RMS_EPS) * g_ref[...]).astype(BF16)

    def mm(name):
        lo, hi = _COLS[name]
        return jnp.dot(u, w_ref[:, lo:hi], preferred_element_type=F32)

    q_ref[0] = (mm("q") * (GLA_DKH ** -0.5)).astype(BF16)
    k_ref[0] = mm("k").astype(BF16)
    v_ref[0] = mm("v").astype(BF16)
    r = mm("r")
    rs_ref[0] = (r * _sigmoid(r)).astype(BF16)
    c1 = mm("c1")
    c2 = mm("c2")
    c_ref[0] = (c1 * _sigmoid(c2)).astype(BF16)
    sgg_ref[0] = _sigmoid(mm("gg")).astype(BF16)
    sgc_ref[0] = _sigmoid(mm("gc")).astype(BF16)
    alr = jnp.dot(u, walr_ref[...], preferred_element_type=F32)
    xa = jnp.dot(alr.astype(BF16), wau_ref[...], preferred_element_type=F32) + bal_ref[...]
    la_ref[0] = _log_sigmoid(xa) * (1.0 / GATE_TAU)


def _proj(x, g, w_main, w_alr, w_au, b_al, tm):
    B, S, D = x.shape
    grid = (B, S // tm)
    row = lambda n: pl.BlockSpec((1, tm, n), lambda b, i: (b, i, 0))
    bf = lambda n: jax.ShapeDtypeStruct((B, S, n), BF16)
    return pl.pallas_call(
        _proj_kernel,
        grid=grid,
        in_specs=[row(D), _const_spec((1, D)), _const_spec((D, N_MAIN)), _const_spec((D, RANK_PAD)),
                  _const_spec((RANK_PAD, GLA_DK)), _const_spec((1, GLA_DK))],
        out_specs=[row(GLA_DK), row(GLA_DK), row(GLA_DV), row(GLA_DV), row(GLA_DK), row(D), row(D), row(D)],
        out_shape=[bf(GLA_DK), bf(GLA_DK), bf(GLA_DV), bf(GLA_DV),
                   jax.ShapeDtypeStruct((B, S, GLA_DK), F32), bf(D), bf(D), bf(D)],
        compiler_params=_params(("parallel", "parallel")),
        name="proj",
    )(x, g, w_main, w_alr, w_au, b_al)


def _gla_kernel(q_ref, k_ref, v_ref, la_ref, rs_ref, gn_ref, s0_ref, og_ref, sfin_ref, st_scr, *, nchunks):
    t = pl.program_id(2)

    @pl.when(t == 0)
    def _():
        st_scr[...] = s0_ref[0]

    i_ = lax.broadcasted_iota(jnp.int32, (CHUNK, CHUNK), 0)
    j_ = lax.broadcasted_iota(jnp.int32, (CHUNK, CHUNK), 1)
    blk_start = jnp.left_shift(jnp.right_shift(i_, 4), 4)
    one = jnp.ones((CHUNK, CHUNK), F32)
    zero = jnp.zeros((CHUNK, CHUNK), F32)
    cum_mat = jnp.concatenate([jnp.where(j_ <= i_, one, zero),
                               jnp.where(j_ < blk_start, one, zero),
                               one], axis=0).astype(BF16)
    col = lax.broadcasted_iota(jnp.int32, (SUB, CHUNK), 1)
    rowi = lax.broadcasted_iota(jnp.int32, (SUB, CHUNK), 0)
    gn = gn_ref[0]

    def chunk_body(c, carry):
        r = pl.multiple_of(c * CHUNK, CHUNK)
        rows = pl.ds(r, CHUNK)
        la = la_ref[0, rows, :]
        la_hi = la.astype(BF16)
        la_lo = (la - la_hi.astype(F32)).astype(BF16)
        cs = jnp.dot(cum_mat, jnp.concatenate([la_hi, la_lo], axis=1), preferred_element_type=F32)
        cs = cs[:, :GLA_DKH] + cs[:, GLA_DKH:]
        b = cs[0:CHUNK]
        bs = cs[CHUNK:2 * CHUNK]
        bl = cs[2 * CHUNK:3 * CHUNK]

        qf = q_ref[0, rows, :].astype(F32)
        kf = k_ref[0, rows, :].astype(F32)
        vb = v_ref[0, rows, :]
        st = st_scr[...]

        q_in = (qf * jnp.exp(b)).astype(BF16)
        o = lax.dot_general(q_in, st.astype(BF16), (((1,), (1,)), ((), ())), preferred_element_type=F32)

        qs = (qf * jnp.exp(b - bs)).astype(BF16)
        a_rows = []
        for blk in range(CHUNK // SUB):
            r0 = blk * SUB
            qb = qf[r0:r0 + SUB]
            kb = kf[r0:r0 + SUB]
            bb = b[r0:r0 + SUB]
            a_blk = jnp.zeros((SUB, CHUNK), F32)
            for j in range(SUB):
                p = qb * kb[j:j + 1] * jnp.exp(jnp.minimum(bb - bb[j:j + 1], 0.0))
                a_blk = jnp.where(col == r0 + j, jnp.sum(p, axis=1, keepdims=True), a_blk)
            a_blk = jnp.where(col <= rowi + r0, a_blk, 0.0)
            if blk > 0:
                kp = (kf * jnp.exp(jnp.minimum(bs[r0:r0 + 1] - b, 0.0))).astype(BF16)
                a_off = lax.dot_general(qs[r0:r0 + SUB], kp, (((1,), (1,)), ((), ())),
                                        preferred_element_type=F32)
                a_blk = a_blk + jnp.where(col < r0, a_off, 0.0)
            a_rows.append(a_blk)
        a = jnp.concatenate(a_rows, axis=0).astype(BF16)
        o = o + jnp.dot(a, vb, preferred_element_type=F32)

        k_st = (kf * jnp.exp(bl - b)).astype(BF16)
        vt = vb.astype(F32).T.astype(BF16)
        st_scr[...] = st * jnp.exp(bl[0:1]) + jnp.dot(vt, k_st, preferred_element_type=F32)

        ms = jnp.mean(o * o, axis=-1, keepdims=True)
        on = o * lax.rsqrt(ms + RMS_EPS) * gn
        og_ref[0, rows, :] = (on * rs_ref[0, rows, :].astype(F32)).astype(BF16)
        return carry

    lax.fori_loop(0, nchunks, chunk_body, 0)

    @pl.when(t == pl.num_programs(2) - 1)
    def _():
        sfin_ref[0, 0] = st_scr[...]


def _gla(q, k, v, la, rs, gn, s0, tb):
    B, S, _ = q.shape
    grid = (B, GLA_HEADS, S // tb)
    hk = pl.BlockSpec((1, tb, GLA_DKH), lambda b, h, t: (b, t, h))
    hv = pl.BlockSpec((1, tb, GLA_DVH), lambda b, h, t: (b, t, h))
    return pl.pallas_call(
        functools.partial(_gla_kernel, nchunks=tb // CHUNK),
        grid=grid,
        in_specs=[hk, hk, hv, hk, hv,
                  pl.BlockSpec((1, 1, GLA_DVH), lambda b, h, t: (h, 0, 0)),
                  pl.BlockSpec((1, GLA_DVH, GLA_DKH), lambda b, h, t: (h, 0, 0))],
        out_specs=[hv, pl.BlockSpec((1, 1, GLA_DVH, GLA_DKH), lambda b, h, t: (b, h, 0, 0))],
        out_shape=[jax.ShapeDtypeStruct((B, S, GLA_DV), BF16),
                   jax.ShapeDtypeStruct((B, GLA_HEADS, GLA_DVH, GLA_DKH), F32)],
        scratch_shapes=[pltpu.VMEM((GLA_DVH, GLA_DKH), F32)],
        compiler_params=_params(("parallel", "parallel", "arbitrary")),
        name="gla",
    )(q, k, v, la, rs, gn, s0)


CONV_RB = 64


def _mix_kernel(c_ref, cprev_ref, ctail0_ref, og_ref, sgg_ref, sgc_ref, x_ref,
                dww_ref, dwb_ref, lng_ref, lnb_ref, wgo_ref, wco_ref, wout_ref,
                h1_ref, win_scr, conv_scr, *, tm):
    i = pl.program_id(1)

    @pl.when(i == 0)
    def _():
        win_scr[0:CONV_HIST, :] = ctail0_ref[...].astype(F32)

    @pl.when(i > 0)
    def _():
        win_scr[0:CONV_HIST, :] = cprev_ref[0].astype(F32)

    win_scr[CONV_HIST:CONV_HIST + tm, :] = c_ref[0].astype(F32)

    off0 = CONV_HIST - (CONF_K - 1)
    for cb in range(D_MODEL // LANES):
        cols = slice(cb * LANES, (cb + 1) * LANES)
        for rb in range(tm // CONV_RB):
            acc = jnp.broadcast_to(dwb_ref[:, cols], (CONV_RB, LANES))
            for j in range(CONF_K):
                start = rb * CONV_RB + off0 + j
                acc = acc + dww_ref[j:j + 1, cols] * win_scr[start:start + CONV_RB, cols]
            conv_scr[rb * CONV_RB:(rb + 1) * CONV_RB, cols] = acc

    cv = conv_scr[...]
    mu = jnp.mean(cv, axis=-1, keepdims=True)
    xc = cv - mu
    var = jnp.mean(xc * xc, axis=-1, keepdims=True)
    y = xc * lax.rsqrt(var + LN_EPS) * lng_ref[...] + lnb_ref[...]
    cs = (y * _sigmoid(y)).astype(BF16)
    br_conf = jnp.dot(cs, wco_ref[...], preferred_element_type=F32)
    br_gla = jnp.dot(og_ref[0], wgo_ref[...], preferred_element_type=F32)
    merged = (sgg_ref[0].astype(F32) * br_gla + sgc_ref[0].astype(F32) * br_conf).astype(BF16)
    h1_ref[0] = x_ref[0] + jnp.dot(merged, wout_ref[...], preferred_element_type=F32)


def _mix(c, ctail0, og, sgg, sgc, x, dww, dwb, lng, lnb, wgo, wco, wout, tm):
    B, S, D = x.shape
    grid = (B, S // tm)
    row = pl.BlockSpec((1, tm, D), lambda b, i: (b, i, 0))
    hist_per_tile = tm // CONV_HIST
    prev = pl.BlockSpec((1, CONV_HIST, D), lambda b, i: (b, jnp.maximum(i * hist_per_tile - 1, 0), 0))
    return pl.pallas_call(
        functools.partial(_mix_kernel, tm=tm),
        grid=grid,
        in_specs=[row, prev, _const_spec((CONV_HIST, D)), row, row, row, row,
                  _const_spec((CONF_K, D)), _const_spec((1, D)), _const_spec((1, D)), _const_spec((1, D)),
                  _const_spec((D, D)), _const_spec((D, D)), _const_spec((D, D))],
        out_specs=row,
        out_shape=jax.ShapeDtypeStruct((B, S, D), F32),
        scratch_shapes=[pltpu.VMEM((CONV_HIST + tm, D), F32), pltpu.VMEM((tm, D), F32)],
        compiler_params=_params(("parallel", "parallel")),
        name="mix",
    )(c, c, ctail0, og, sgg, sgc, x, dww, dwb, lng, lnb, wgo, wco, wout)


def _ffn_kernel(h1_ref, g_ref, wup_ref, dww_ref, dwb_ref, wdn_ref, gfin_ref, atail0_ref,
                out_ref, atail_ref, win_scr, *, tm):
    i = pl.program_id(1)

    @pl.when(i == 0)
    def _():
        win_scr[0:FFN_HIST, :] = atail0_ref[...]

    h1 = h1_ref[0]
    ms = jnp.mean(h1 * h1, axis=-1, keepdims=True)
    u = (h1 * lax.rsqrt(ms + RMS_EPS) * g_ref[...]).astype(BF16)
    a = jnp.dot(u, wup_ref[:, 0:D_FF], preferred_element_type=F32)
    bv = jnp.dot(u, wup_ref[:, D_FF:2 * D_FF], preferred_element_type=F32)
    win_scr[FFN_HIST:FFN_HIST + tm, :] = a
    off0 = FFN_HIST - (FFN_K - 1)
    ac = dwb_ref[...] + dww_ref[0:1, :] * win_scr[off0:off0 + tm, :]
    for j in range(1, FFN_K):
        ac = ac + dww_ref[j:j + 1, :] * win_scr[off0 + j:off0 + j + tm, :]
    hdn = (ac * _sigmoid(ac) * bv).astype(BF16)
    h2 = h1 + jnp.dot(hdn, wdn_ref[...], preferred_element_type=F32)
    ms2 = jnp.mean(h2 * h2, axis=-1, keepdims=True)
    out_ref[0] = h2 * lax.rsqrt(ms2 + RMS_EPS) * gfin_ref[...]
    tail = win_scr[tm:tm + FFN_HIST, :]
    win_scr[0:FFN_HIST, :] = tail
    atail_ref[0] = tail


def _ffn(h1, g, wup, dww, dwb, wdn, gfin, atail0, tm):
    B, S, D = h1.shape
    grid = (B, S // tm)
    row = pl.BlockSpec((1, tm, D), lambda b, i: (b, i, 0))
    return pl.pallas_call(
        functools.partial(_ffn_kernel, tm=tm),
        grid=grid,
        in_specs=[row, _const_spec((1, D)), _const_spec((D, 2 * D_FF)), _const_spec((FFN_K, D_FF)),
                  _const_spec((1, D_FF)), _const_spec((D_FF, D)), _const_spec((1, D)),
                  _const_spec((FFN_HIST, D_FF))],
        out_specs=[row, pl.BlockSpec((1, FFN_HIST, D_FF), lambda b, i: (b, 0, 0))],
        out_shape=[jax.ShapeDtypeStruct((B, S, D), F32), jax.ShapeDtypeStruct((B, FFN_HIST, D_FF), F32)],
        scratch_shapes=[pltpu.VMEM((FFN_HIST + tm, D_FF), F32)],
        compiler_params=_params(("parallel", "arbitrary")),
        name="ffn",
    )(h1, g, wup, dww, dwb, wdn, gfin, atail0)


def _block(h, carries, wts, tiles, la_mask_rows=0):
    s0, ctail0, atail0 = carries
    tm_proj, tb_gla, tm_mix, tm_ffn = tiles
    q, k, v, rs, la, c, sgg, sgc = _proj(h, wts["g_mix"], wts["w_main"], wts["w_alr"], wts["w_au"],
                                         wts["b_al"], tm_proj)
    if la_mask_rows:
        la = la.at[:, :la_mask_rows].set(0.0)
    og, sfin = _gla(q, k, v, la, rs, wts["g_gla"], s0, tb_gla)
    h1 = _mix(c, ctail0, og, sgg, sgc, h, wts["dw_w"], wts["dw_b"], wts["ln_g"], wts["ln_b"],
              wts["w_gla_o"], wts["w_conf_o"], wts["w_out"], tm_mix)
    out, atail = _ffn(h1, wts["g_ffn"], wts["w_up"], wts["ffn_dw_w"], wts["ffn_dw_b"], wts["w_down"],
                      wts["g_fin"], atail0, tm_ffn)
    return out, (sfin[0], c[0, -CONV_HIST:], atail[0])


def kernel(x, meta_tokens, norm_mix_g, w_in, w_alpha_up, b_alpha, gla_norm_g, w_gla_o, conf_dw_w, conf_dw_b,
           conf_ln_g, conf_ln_b, w_conf_o, w_out, norm_ffn_g, w_up, ffn_dw_w, ffn_dw_b, w_down, final_norm_g):
    assert norm_mix_g.shape[0] == 1, "single-layer block"
    B, S, D = x.shape
    lr0 = 2 * GLA_DK + 2 * GLA_DV
    w_in0 = w_in[0]
    wts = dict(
        g_mix=norm_mix_g[0][None],
        w_main=jnp.concatenate([w_in0[:, :lr0], w_in0[:, lr0 + GLA_RANK:]], axis=1).astype(BF16),
        w_alr=jnp.pad(w_in0[:, lr0:lr0 + GLA_RANK], ((0, 0), (0, RANK_PAD - GLA_RANK))).astype(BF16),
        w_au=jnp.pad(w_alpha_up[0], ((0, RANK_PAD - GLA_RANK), (0, 0))).astype(BF16),
        b_al=b_alpha[0][None],
        g_gla=gla_norm_g[0].reshape(GLA_HEADS, 1, GLA_DVH),
        dw_w=conf_dw_w[0], dw_b=conf_dw_b[0][None], ln_g=conf_ln_g[0][None], ln_b=conf_ln_b[0][None],
        w_gla_o=w_gla_o[0].astype(BF16), w_conf_o=w_conf_o[0].astype(BF16), w_out=w_out[0].astype(BF16),
        g_ffn=norm_ffn_g[0][None], w_up=w_up[0].astype(BF16),
        ffn_dw_w=ffn_dw_w[0], ffn_dw_b=ffn_dw_b[0][None], w_down=w_down[0].astype(BF16),
        g_fin=final_norm_g[None],
    )
    zero_carries = (jnp.zeros((GLA_HEADS, GLA_DVH, GLA_DKH), F32),
                    jnp.zeros((CONV_HIST, D), BF16),
                    jnp.zeros((FFN_HIST, D_FF), F32))
    h_meta = jnp.concatenate([jnp.zeros((META_PAD, D), x.dtype), meta_tokens.astype(x.dtype)], axis=0)[None]
    _, carries = _block(h_meta, zero_carries, wts, (CHUNK, CHUNK, CHUNK, CHUNK), la_mask_rows=META_PAD)
    out, _ = _block(x, carries, wts, (512, 512, 256, 256))
    return out
```

```python
import functools

import jax
import jax.numpy as jnp
from jax import lax
from jax.experimental import pallas as pl
from jax.experimental.pallas import tpu as pltpu

F32 = jnp.float32
BF16 = jnp.bfloat16

D_MODEL = 1024
N_META = 16
GLA_HEADS = 4
GLA_DK = 512
GLA_DV = 1024
GLA_DKH = GLA_DK // GLA_HEADS
GLA_DVH = GLA_DV // GLA_HEADS
GLA_RANK = 16
GATE_TAU = 16.0
LOG2_E = 1.4426950408889634
CHUNK = 64
META_PAD = CHUNK - N_META
CONF_K = 31
CONV_HIST = 32
D_FF = 2816
FFN_K = 3
FFN_HIST = 8
RMS_EPS = 1e-6
LN_EPS = 1e-5
LANES = 128
SUBLANES = 8
RANK_PAD = LANES
VMEM_LIMIT = 56 * 1024 * 1024

_COLS = dict(q=(0, 512), k=(512, 1024), v=(1024, 2048), r=(2048, 3072),
             c1=(3072, 4096), c2=(4096, 5120), gg=(5120, 6144), gc=(6144, 7168))
N_MAIN = 7168


def _sigmoid(x):
    return 1.0 / (1.0 + jnp.exp(-x))


def _log_sigmoid(x):
    return jnp.minimum(x, 0.0) - jnp.log(1.0 + jnp.exp(-jnp.abs(x)))


def _const_spec(shape):
    nd = len(shape)
    return pl.BlockSpec(shape, lambda *_: (0,) * nd, pipeline_mode=pl.Buffered(1))


def _params(sem):
    return pltpu.CompilerParams(dimension_semantics=sem, vmem_limit_bytes=VMEM_LIMIT)


def _proj_kernel(x_ref, g_ref, w_ref, walr_ref, wau_ref, bal_ref,
                 q_ref, k_ref, v_ref, rs_ref, la_ref, c_ref, sgg_ref, sgc_ref):
    x = x_ref[0]
    ms = jnp.mean(x * x, axis=-1, keepdims=True)
    u = (x * lax.rsqrt(ms + RMS_EPS) * g_ref[...]).astype(BF16)

    def mm(name):
        lo, hi = _COLS[name]
        return jnp.dot(u, w_ref[:, lo:hi], preferred_element_type=F32)

    q_ref[0] = (mm("q") * (GLA_DKH ** -0.5)).astype(BF16)
    k_ref[0] = mm("k").astype(BF16)
    v_ref[0] = mm("v").astype(BF16)
    r = mm("r")
    rs_ref[0] = (r * _sigmoid(r)).astype(BF16)
    c1 = mm("c1")
    c2 = mm("c2")
    c_ref[0] = (c1 * _sigmoid(c2)).astype(BF16)
    sgg_ref[0] = _sigmoid(mm("gg")).astype(BF16)
    sgc_ref[0] = _sigmoid(mm("gc")).astype(BF16)
    alr = jnp.dot(u, walr_ref[...], preferred_element_type=F32)
    xa = jnp.dot(alr.astype(BF16), wau_ref[...], preferred_element_type=F32) + bal_ref[...]
    la_ref[0] = _log_sigmoid(xa) * (1.0 / GATE_TAU)


def _proj(x, g, w_main, w_alr, w_au, b_al, tm):
    B, S, D = x.shape
    grid = (B, S // tm)
    row = lambda n: pl.BlockSpec((1, tm, n), lambda b, i: (b, i, 0))
    bf = lambda n: jax.ShapeDtypeStruct((B, S, n), BF16)
    return pl.pallas_call(
        _proj_kernel,
        grid=grid,
        in_specs=[row(D), _const_spec((1, D)), _const_spec((D, N_MAIN)), _const_spec((D, RANK_PAD)),
                  _const_spec((RANK_PAD, GLA_DK)), _const_spec((1, GLA_DK))],
        out_specs=[row(GLA_DK), row(GLA_DK), row(GLA_DV), row(GLA_DV), row(GLA_DK), row(D), row(D), row(D)],
        out_shape=[bf(GLA_DK), bf(GLA_DK), bf(GLA_DV), bf(GLA_DV),
                   jax.ShapeDtypeStruct((B, S, GLA_DK), F32), bf(D), bf(D), bf(D)],
        compiler_params=_params(("parallel", "parallel")),
        name="proj",
    )(x, g, w_main, w_alr, w_au, b_al)


SUB = 8


def _gla_masks():
    i_ = lax.broadcasted_iota(jnp.int32, (CHUNK, CHUNK), 0)
    j_ = lax.broadcasted_iota(jnp.int32, (CHUNK, CHUNK), 1)
    sh = lambda x, n: jnp.right_shift(x, n)
    one = jnp.ones((CHUNK, CHUNK), F32)
    zero = jnp.zeros((CHUNK, CHUNK), F32)
    sel = lambda m: jnp.where(m, one, zero)
    cum_mat = jnp.concatenate([
        sel(j_ <= i_),
        sel(j_ < 32),
        sel(j_ < jnp.left_shift(sh(i_, 5), 5) + 16),
        sel(j_ < jnp.left_shift(sh(i_, 4), 4) + 8),
        one,
    ], axis=0).astype(BF16)
    up = lambda x, n: jnp.bitwise_and(sh(x, n), 1) == 1
    m32 = sel(up(i_, 5)) * sel(~up(j_, 5))
    m16 = sel(up(i_, 4)) * sel(~up(j_, 4)) * sel(sh(i_, 5) == sh(j_, 5))
    m8 = sel(up(i_, 3)) * sel(~up(j_, 3)) * sel(sh(i_, 4) == sh(j_, 4))
    md = sel(sh(i_, 3) == sh(j_, 3)) * sel(j_ <= i_)
    return cum_mat, (m32 > 0.5, m16 > 0.5, m8 > 0.5, md > 0.5)


def _nt_dot(a, b):
    return lax.dot_general(a, b, (((1,), (1,)), ((), ())), preferred_element_type=F32)


def _gla_head(qf, kf, vb, la, st, cum_mat, masks, nch, row_scr):
    m32, m16, m8, md = masks
    la = la * LOG2_E
    la_hi = la.astype(BF16)
    la_lo = (la - la_hi.astype(F32)).astype(BF16)
    la2 = jnp.concatenate([la_hi, la_lo], axis=1)
    parts = [[] for _ in range(5)]
    for c in range(nch):
        cs = jnp.dot(cum_mat, la2[c * CHUNK:(c + 1) * CHUNK], preferred_element_type=F32)
        cs = cs[:, :GLA_DKH] + cs[:, GLA_DKH:]
        for n in range(5):
            parts[n].append(cs[n * CHUNK:(n + 1) * CHUNK])
    b, p32, p16, p8, bl = (jnp.concatenate(p, axis=0) for p in parts)

    q_in = (qf * jnp.exp2(b)).astype(BF16)
    k_st = (kf * jnp.exp2(bl - b)).astype(BF16)
    q32 = (qf * jnp.exp2(b - p32)).astype(BF16)
    k32 = (kf * jnp.exp2(p32 - b)).astype(BF16)
    q16 = (qf * jnp.exp2(b - p16)).astype(BF16)
    k16 = (kf * jnp.exp2(p16 - b)).astype(BF16)
    q8 = (qf * jnp.exp2(b - p8)).astype(BF16)
    k8 = (kf * jnp.exp2(p8 - b)).astype(BF16)
    col = lax.broadcasted_iota(jnp.int32, (SUB, CHUNK), 1)
    row_scr[0] = kf
    row_scr[1] = b

    outs = []
    for c in range(nch):
        rs_ = slice(c * CHUNK, (c + 1) * CHUNK)
        diag = []
        for g in range(CHUNK // SUB):
            r0 = c * CHUNK + g * SUB
            qb = qf[r0:r0 + SUB]
            bb = b[r0:r0 + SUB]
            a_g = jnp.zeros((SUB, CHUNK), F32)
            for j in range(SUB):
                k_j = row_scr[0, r0 + j:r0 + j + 1, :]
                b_j = row_scr[1, r0 + j:r0 + j + 1, :]
                p = qb * k_j * jnp.exp2(bb - b_j)
                a_g = jnp.where(col == g * SUB + j, jnp.sum(p, axis=1, keepdims=True), a_g)
            diag.append(a_g)
        a = jnp.where(md, jnp.concatenate(diag, axis=0), 0.0)
        a = jnp.where(m8, _nt_dot(q8[rs_], k8[rs_]), a)
        a = jnp.where(m16, _nt_dot(q16[rs_], k16[rs_]), a)
        a = jnp.where(m32, _nt_dot(q32[rs_], k32[rs_]), a)
        v_c = vb[rs_]
        o = jnp.dot(a.astype(BF16), v_c, preferred_element_type=F32) + _nt_dot(q_in[rs_], st.astype(BF16))
        outs.append(o)
        vt = v_c.astype(F32).T.astype(BF16)
        st = st * jnp.exp2(bl[c * CHUNK:c * CHUNK + 1]) + jnp.dot(vt, k_st[rs_], preferred_element_type=F32)
    return jnp.concatenate(outs, axis=0), st


def _gla_kernel(q_ref, k_ref, v_ref, la_ref, rs_ref, gn_ref, s0_ref, og_ref, sfin_ref, st_scr, row_scr, *, nch):
    t = pl.program_id(1)

    @pl.when(t == 0)
    def _():
        st_scr[...] = s0_ref[...]

    cum_mat, masks = _gla_masks()
    for h in range(GLA_HEADS):
        kc = slice(h * GLA_DKH, (h + 1) * GLA_DKH)
        vc = slice(h * GLA_DVH, (h + 1) * GLA_DVH)
        o, st = _gla_head(q_ref[0, :, kc].astype(F32), k_ref[0, :, kc].astype(F32), v_ref[0, :, vc],
                          la_ref[0, :, kc], st_scr[h], cum_mat, masks, nch, row_scr.at[h])
        st_scr[h] = st
        ms = jnp.mean(o * o, axis=-1, keepdims=True)
        on = o * lax.rsqrt(ms + RMS_EPS) * gn_ref[h]
        og_ref[0, :, vc] = (on * rs_ref[0, :, vc].astype(F32)).astype(BF16)

    @pl.when(t == pl.num_programs(1) - 1)
    def _():
        sfin_ref[0] = st_scr[...]


def _gla(q, k, v, la, rs, gn, s0, tb):
    B, S, _ = q.shape
    grid = (B, S // tb)
    rk = pl.BlockSpec((1, tb, GLA_DK), lambda b, t: (b, t, 0))
    rv = pl.BlockSpec((1, tb, GLA_DV), lambda b, t: (b, t, 0))
    st_shape = (GLA_HEADS, GLA_DVH, GLA_DKH)
    return pl.pallas_call(
        functools.partial(_gla_kernel, nch=tb // CHUNK),
        grid=grid,
        in_specs=[rk, rk, rv, rk, rv, _const_spec((GLA_HEADS, 1, GLA_DVH)), _const_spec(st_shape)],
        out_specs=[rv, pl.BlockSpec((1,) + st_shape, lambda b, t: (b, 0, 0, 0))],
        out_shape=[jax.ShapeDtypeStruct((B, S, GLA_DV), BF16), jax.ShapeDtypeStruct((B,) + st_shape, F32)],
        scratch_shapes=[pltpu.VMEM(st_shape, F32), pltpu.VMEM((GLA_HEADS, 2, tb, GLA_DKH), F32)],
        compiler_params=_params(("parallel", "arbitrary")),
        name="gla",
    )(q, k, v, la, rs, gn, s0)


CONV_RB = 64


def _mix_kernel(c_ref, cprev_ref, ctail0_ref, og_ref, sgg_ref, sgc_ref, x_ref,
                dww_ref, dwb_ref, lng_ref, lnb_ref, wgo_ref, wco_ref, wout_ref,
                h1_ref, win_scr, conv_scr, *, tm):
    i = pl.program_id(1)
    nw = CONV_HIST + tm

    @pl.when(i == 0)
    def _():
        win_scr[0, 0:CONV_HIST, :] = ctail0_ref[...].astype(F32)

    @pl.when(i > 0)
    def _():
        win_scr[0, 0:CONV_HIST, :] = cprev_ref[0].astype(F32)

    win_scr[0, CONV_HIST:nw, :] = c_ref[0].astype(F32)
    win = win_scr[0]
    for r in range(1, SUBLANES):
        win_scr[r] = pltpu.roll(win, nw - r, 0)

    off0 = CONV_HIST - (CONF_K - 1)
    for cb in range(D_MODEL // LANES):
        cols = slice(cb * LANES, (cb + 1) * LANES)
        for rb in range(tm // CONV_RB):
            acc = jnp.broadcast_to(dwb_ref[:, cols], (CONV_RB, LANES))
            for j in range(CONF_K):
                r = (off0 + j) % SUBLANES
                start = rb * CONV_RB + (off0 + j) - r
                acc = acc + dww_ref[j:j + 1, cols] * win_scr[r, start:start + CONV_RB, cols]
            conv_scr[rb * CONV_RB:(rb + 1) * CONV_RB, cols] = acc

    cv = conv_scr[...]
    mu = jnp.mean(cv, axis=-1, keepdims=True)
    xc = cv - mu
    var = jnp.mean(xc * xc, axis=-1, keepdims=True)
    y = xc * lax.rsqrt(var + LN_EPS) * lng_ref[...] + lnb_ref[...]
    cs = (y * _sigmoid(y)).astype(BF16)
    br_conf = jnp.dot(cs, wco_ref[...], preferred_element_type=F32)
    br_gla = jnp.dot(og_ref[0], wgo_ref[...], preferred_element_type=F32)
    merged = (sgg_ref[0].astype(F32) * br_gla + sgc_ref[0].astype(F32) * br_conf).astype(BF16)
    h1_ref[0] = x_ref[0] + jnp.dot(merged, wout_ref[...], preferred_element_type=F32)


def _mix(c, ctail0, og, sgg, sgc, x, dww, dwb, lng, lnb, wgo, wco, wout, tm):
    B, S, D = x.shape
    grid = (B, S // tm)
    row = pl.BlockSpec((1, tm, D), lambda b, i: (b, i, 0))
    hist_per_tile = tm // CONV_HIST
    prev = pl.BlockSpec((1, CONV_HIST, D), lambda b, i: (b, jnp.maximum(i * hist_per_tile - 1, 0), 0))
    return pl.pallas_call(
        functools.partial(_mix_kernel, tm=tm),
        grid=grid,
        in_specs=[row, prev, _const_spec((CONV_HIST, D)), row, row, row, row,
                  _const_spec((CONF_K, D)), _const_spec((1, D)), _const_spec((1, D)), _const_spec((1, D)),
                  _const_spec((D, D)), _const_spec((D, D)), _const_spec((D, D))],
        out_specs=row,
        out_shape=jax.ShapeDtypeStruct((B, S, D), F32),
        scratch_shapes=[pltpu.VMEM((SUBLANES, CONV_HIST + tm, D), F32), pltpu.VMEM((tm, D), F32)],
        compiler_params=_params(("parallel", "parallel")),
        name="mix",
    )(c, c, ctail0, og, sgg, sgc, x, dww, dwb, lng, lnb, wgo, wco, wout)


def _ffn_kernel(h1_ref, g_ref, wup_ref, dww_ref, dwb_ref, wdn_ref, gfin_ref, atail0_ref,
                out_ref, atail_ref, win_scr, *, tm):
    i = pl.program_id(1)

    @pl.when(i == 0)
    def _():
        win_scr[0:FFN_HIST, :] = atail0_ref[...]

    h1 = h1_ref[0]
    ms = jnp.mean(h1 * h1, axis=-1, keepdims=True)
    u = (h1 * lax.rsqrt(ms + RMS_EPS) * g_ref[...]).astype(BF16)
    a = jnp.dot(u, wup_ref[:, 0:D_FF], preferred_element_type=F32)
    bv = jnp.dot(u, wup_ref[:, D_FF:2 * D_FF], preferred_element_type=F32)
    win_scr[FFN_HIST:FFN_HIST + tm, :] = a
    off0 = FFN_HIST - (FFN_K - 1)
    ac = dwb_ref[...] + dww_ref[0:1, :] * win_scr[off0:off0 + tm, :]
    for j in range(1, FFN_K):
        ac = ac + dww_ref[j:j + 1, :] * win_scr[off0 + j:off0 + j + tm, :]
    hdn = (ac * _sigmoid(ac) * bv).astype(BF16)
    h2 = h1 + jnp.dot(hdn, wdn_ref[...], preferred_element_type=F32)
    ms2 = jnp.mean(h2 * h2, axis=-1, keepdims=True)
    out_ref[0] = h2 * lax.rsqrt(ms2 + RMS_EPS) * gfin_ref[...]
    tail = win_scr[tm:tm + FFN_HIST, :]
    win_scr[0:FFN_HIST, :] = tail
    atail_ref[0] = tail


def _ffn(h1, g, wup, dww, dwb, wdn, gfin, atail0, tm):
    B, S, D = h1.shape
    grid = (B, S // tm)
    row = pl.BlockSpec((1, tm, D), lambda b, i: (b, i, 0))
    return pl.pallas_call(
        functools.partial(_ffn_kernel, tm=tm),
        grid=grid,
        in_specs=[row, _const_spec((1, D)), _const_spec((D, 2 * D_FF)), _const_spec((FFN_K, D_FF)),
                  _const_spec((1, D_FF)), _const_spec((D_FF, D)), _const_spec((1, D)),
                  _const_spec((FFN_HIST, D_FF))],
        out_specs=[row, pl.BlockSpec((1, FFN_HIST, D_FF), lambda b, i: (b, 0, 0))],
        out_shape=[jax.ShapeDtypeStruct((B, S, D), F32), jax.ShapeDtypeStruct((B, FFN_HIST, D_FF), F32)],
        scratch_shapes=[pltpu.VMEM((FFN_HIST + tm, D_FF), F32)],
        compiler_params=_params(("parallel", "arbitrary")),
        name="ffn",
    )(h1, g, wup, dww, dwb, wdn, gfin, atail0)


def _block(h, carries, wts, tiles, la_mask_rows=0):
    s0, ctail0, atail0 = carries
    tm_proj, tb_gla, tm_mix, tm_ffn = tiles
    q, k, v, rs, la, c, sgg, sgc = _proj(h, wts["g_mix"], wts["w_main"], wts["w_alr"], wts["w_au"],
                                         wts["b_al"], tm_proj)
    if la_mask_rows:
        la = la.at[:, :la_mask_rows].set(0.0)
    og, sfin = _gla(q, k, v, la, rs, wts["g_gla"], s0, tb_gla)
    h1 = _mix(c, ctail0, og, sgg, sgc, h, wts["dw_w"], wts["dw_b"], wts["ln_g"], wts["ln_b"],
              wts["w_gla_o"], wts["w_conf_o"], wts["w_out"], tm_mix)
    out, atail = _ffn(h1, wts["g_ffn"], wts["w_up"], wts["ffn_dw_w"], wts["ffn_dw_b"], wts["w_down"],
                      wts["g_fin"], atail0, tm_ffn)
    return out, (sfin[0], c[0, -CONV_HIST:], atail[0])


def kernel(x, meta_tokens, norm_mix_g, w_in, w_alpha_up, b_alpha, gla_norm_g, w_gla_o, conf_dw_w, conf_dw_b,
           conf_ln_g, conf_ln_b, w_conf_o, w_out, norm_ffn_g, w_up, ffn_dw_w, ffn_dw_b, w_down, final_norm_g):
    assert norm_mix_g.shape[0] == 1, "single-layer block"
    B, S, D = x.shape
    lr0 = 2 * GLA_DK + 2 * GLA_DV
    w_in0 = w_in[0]
    wts = dict(
        g_mix=norm_mix_g[0][None],
        w_main=jnp.concatenate([w_in0[:, :lr0], w_in0[:, lr0 + GLA_RANK:]], axis=1).astype(BF16),
        w_alr=jnp.pad(w_in0[:, lr0:lr0 + GLA_RANK], ((0, 0), (0, RANK_PAD - GLA_RANK))).astype(BF16),
        w_au=jnp.pad(w_alpha_up[0], ((0, RANK_PAD - GLA_RANK), (0, 0))).astype(BF16),
        b_al=b_alpha[0][None],
        g_gla=gla_norm_g[0].reshape(GLA_HEADS, 1, GLA_DVH),
        dw_w=conf_dw_w[0], dw_b=conf_dw_b[0][None], ln_g=conf_ln_g[0][None], ln_b=conf_ln_b[0][None],
        w_gla_o=w_gla_o[0].astype(BF16), w_conf_o=w_conf_o[0].astype(BF16), w_out=w_out[0].astype(BF16),
        g_ffn=norm_ffn_g[0][None], w_up=w_up[0].astype(BF16),
        ffn_dw_w=ffn_dw_w[0], ffn_dw_b=ffn_dw_b[0][None], w_down=w_down[0].astype(BF16),
        g_fin=final_norm_g[None],
    )
    zero_carries = (jnp.zeros((GLA_HEADS, GLA_DVH, GLA_DKH), F32),
                    jnp.zeros((CONV_HIST, D), BF16),
                    jnp.zeros((FFN_HIST, D_FF), F32))
    h_meta = jnp.concatenate([jnp.zeros((META_PAD, D), x.dtype), meta_tokens.astype(x.dtype)], axis=0)[None]
    _, carries = _block(h_meta, zero_carries, wts, (CHUNK, CHUNK, CHUNK, CHUNK), la_mask_rows=META_PAD)
    out, _ = _block(x, carries, wts, (512, 256, 256, 256))
    return out
```

```python
import functools

import jax
import jax.numpy as jnp
from jax import lax
from jax.experimental import pallas as pl
from jax.experimental.pallas import tpu as pltpu

F32 = jnp.float32
BF16 = jnp.bfloat16

D_MODEL = 1024
N_META = 16
GLA_HEADS = 4
GLA_DK = 512
GLA_DV = 1024
GLA_DKH = GLA_DK // GLA_HEADS
GLA_DVH = GLA_DV // GLA_HEADS
GLA_RANK = 16
GATE_TAU = 16.0
LOG2_E = 1.4426950408889634
CHUNK = 64
SUB = 8
META_PAD = CHUNK - N_META
CONF_K = 31
CONV_HIST = 32
CONV_RB = 64
D_FF = 2816
FFN_K = 3
FFN_HIST = 8
RMS_EPS = 1e-6
LN_EPS = 1e-5
LANES = 128
SUBLANES = 8
RANK_PAD = LANES
VMEM_LIMIT = 60 * 1024 * 1024

_COLS = dict(q=(0, 512), k=(512, 1024), v=(1024, 2048), r=(2048, 3072),
             c1=(3072, 4096), c2=(4096, 5120), gg=(5120, 6144), gc=(6144, 7168))
N_MAIN = 7168


def _sigmoid(x):
    return 1.0 / (1.0 + jnp.exp(-x))


def _log_sigmoid(x):
    return jnp.minimum(x, 0.0) - jnp.log(1.0 + jnp.exp(-jnp.abs(x)))


def _const_spec(shape):
    nd = len(shape)
    return pl.BlockSpec(shape, lambda *_: (0,) * nd, pipeline_mode=pl.Buffered(1))


def _params(sem):
    return pltpu.CompilerParams(dimension_semantics=sem, vmem_limit_bytes=VMEM_LIMIT)


def _zero_after(v):
    bits = lax.bitcast_convert_type(v, jnp.uint32)
    half_word = jnp.uint32(16)
    bits = lax.shift_right_logical(lax.shift_right_logical(bits, half_word), half_word)
    return lax.bitcast_convert_type(bits, F32)


def _conv31_block(cb, win_scr, sh_scr, dww_ref, dwb_ref, conv_scr, tm, after=None):
    nw = CONV_HIST + tm
    off0 = CONV_HIST - (CONF_K - 1)
    cols = slice(cb * LANES, (cb + 1) * LANES)
    sh = sh_scr.at[cb % 2]
    wblk = win_scr[:, cols]
    for r in range(1, SUBLANES):
        sh[r - 1] = pltpu.roll(wblk, nw - r, 0)
    bias = jnp.broadcast_to(dwb_ref[:, cols], (SUBLANES, LANES))
    if after is not None:
        bias = bias + _zero_after(after)
    for rb in range(tm // CONV_RB):
        acc = jnp.concatenate([bias] * (CONV_RB // SUBLANES), axis=0)
        for j in range(CONF_K):
            r = (off0 + j) % SUBLANES
            start = rb * CONV_RB + (off0 + j) - r
            if r == 0:
                src = win_scr[start:start + CONV_RB, cols]
            else:
                src = sh[r - 1, start:start + CONV_RB, :]
            acc = acc + dww_ref[j:j + 1, cols] * src
        conv_scr[rb * CONV_RB:(rb + 1) * CONV_RB, cols] = acc
    return acc[CONV_RB - SUBLANES:CONV_RB]


def _proj_kernel(x_ref, g_ref, w_ref, walr_ref, wau_ref, bal_ref, dww_ref, dwb_ref, lng_ref, lnb_ref, ctail0_ref,
                 q_ref, k_ref, v_ref, rs_ref, la_ref, cs_ref, sgg_ref, sgc_ref, ctail_ref,
                 win_scr, sh_scr, conv_scr, *, tm):
    i = pl.program_id(1)

    @pl.when(i == 0)
    def _():
        win_scr[0:CONV_HIST, :] = ctail0_ref[...]

    x = x_ref[0]
    ms = jnp.mean(x * x, axis=-1, keepdims=True)
    u = (x * lax.rsqrt(ms + RMS_EPS) * g_ref[...]).astype(BF16)

    def mm(name, part=0, nparts=1):
        lo, hi = _COLS[name]
        w = (hi - lo) // nparts
        return jnp.dot(u, w_ref[:, lo + part * w:lo + (part + 1) * w], preferred_element_type=F32)

    def half(n, part):
        return slice(part * (n // 2), (part + 1) * (n // 2))

    def last_tile(res, after):
        tile = res[tm - SUBLANES:tm, 0:LANES]
        return tile if after is None else tile + _zero_after(after)

    n_glu = 4
    glu_w = D_MODEL // n_glu
    n_blocks = D_MODEL // LANES
    n_norm = 4
    norm_rows = tm // n_norm

    def p_glu(part, after):
        c1 = mm("c1", part, n_glu)
        c2 = mm("c2", part, n_glu)
        win_scr[CONV_HIST:CONV_HIST + tm, part * glu_w:(part + 1) * glu_w] = c1 * _sigmoid(c2)
        return last_tile(c2, after)

    def p_q(after):
        res = mm("q")
        q_ref[0] = (res * (GLA_DKH ** -0.5)).astype(BF16)
        return last_tile(res, after)

    def p_k(after):
        res = mm("k")
        k_ref[0] = res.astype(BF16)
        return last_tile(res, after)

    def p_v(part, after):
        res = mm("v", part, 2)
        v_ref[0, :, half(GLA_DV, part)] = res.astype(BF16)
        return last_tile(res, after)

    def p_r(part, after):
        r = mm("r", part, 2)
        rs_ref[0, :, half(GLA_DV, part)] = (r * _sigmoid(r)).astype(BF16)
        return last_tile(r, after)

    def p_gate(name, ref, part, after):
        res = mm(name, part, 2)
        ref[0, :, half(D_MODEL, part)] = _sigmoid(res).astype(BF16)
        return last_tile(res, after)

    def p_la(after):
        alr = jnp.dot(u, walr_ref[...], preferred_element_type=F32)
        xa = jnp.dot(alr.astype(BF16), wau_ref[...], preferred_element_type=F32) + bal_ref[...]
        la_ref[0] = _log_sigmoid(xa) * (1.0 / GATE_TAU)
        return last_tile(xa, after)

    def u_conv(cb, after):
        return _conv31_block(cb, win_scr, sh_scr, dww_ref, dwb_ref, conv_scr, tm, after=after)

    def u_norm(part, after):
        rows = slice(part * norm_rows, (part + 1) * norm_rows)
        zero_row = jnp.concatenate([_zero_after(after)[0:1]] * n_blocks, axis=1)
        cv = conv_scr[rows, :] + zero_row
        mu = jnp.mean(cv, axis=-1, keepdims=True)
        xc = cv - mu
        var = jnp.mean(xc * xc, axis=-1, keepdims=True)
        y = xc * lax.rsqrt(var + LN_EPS) * lng_ref[...] + lnb_ref[...]
        cs_ref[0, rows, :] = (y * _sigmoid(y)).astype(BF16)
        return y[norm_rows - SUBLANES:norm_rows, 0:LANES]

    P = functools.partial
    pieces = [P(p_glu, 0), P(p_glu, 1), p_la, P(p_glu, 2), P(p_glu, 3), p_q, p_k, P(p_v, 0), P(p_v, 1),
              P(p_r, 0), P(p_r, 1), P(p_gate, "gg", sgg_ref, 0), P(p_gate, "gg", sgg_ref, 1),
              P(p_gate, "gc", sgc_ref, 0), P(p_gate, "gc", sgc_ref, 1)]
    units = [P(u_conv, cb) for cb in range(n_blocks)] + [P(u_norm, part) for part in range(n_norm)]
    groups = [[0], [1], [2, 3], [4], [5, 6], [7], [8, 9], [10], [11], [12], [13], [14]]
    folds = {groups[j + 2][-1]: j for j in range(len(units) - 2)}
    unit_tiles = []
    for j, unit in enumerate(units):
        tiles = [pieces[m](unit_tiles[folds[m]] if m in folds else None) for m in groups[j]]
        unit_tiles.append(unit(functools.reduce(lambda a, b: a + b, tiles)))
        if j == n_blocks - 1:
            tail = win_scr[tm:tm + CONV_HIST, :]
            win_scr[0:CONV_HIST, :] = tail
            ctail_ref[0] = tail


def _proj(x, g, w_main, w_alr, w_au, b_al, dww, dwb, lng, lnb, ctail0, tm):
    B, S, D = x.shape
    grid = (B, S // tm)
    row = lambda n: pl.BlockSpec((1, tm, n), lambda b, i: (b, i, 0))
    bf = lambda n: jax.ShapeDtypeStruct((B, S, n), BF16)
    return pl.pallas_call(
        functools.partial(_proj_kernel, tm=tm),
        grid=grid,
        in_specs=[row(D), _const_spec((1, D)), _const_spec((D, N_MAIN)), _const_spec((D, RANK_PAD)),
                  _const_spec((RANK_PAD, GLA_DK)), _const_spec((1, GLA_DK)),
                  _const_spec((CONF_K, D)), _const_spec((1, D)), _const_spec((1, D)), _const_spec((1, D)),
                  _const_spec((CONV_HIST, D))],
        out_specs=[row(GLA_DK), row(GLA_DK), row(GLA_DV), row(GLA_DV), row(GLA_DK), row(D), row(D), row(D),
                   pl.BlockSpec((1, CONV_HIST, D), lambda b, i: (b, 0, 0))],
        out_shape=[bf(GLA_DK), bf(GLA_DK), bf(GLA_DV), bf(GLA_DV),
                   jax.ShapeDtypeStruct((B, S, GLA_DK), F32), bf(D), bf(D), bf(D),
                   jax.ShapeDtypeStruct((B, CONV_HIST, D), F32)],
        scratch_shapes=[pltpu.VMEM((CONV_HIST + tm, D), F32),
                        pltpu.VMEM((2, SUBLANES - 1, CONV_HIST + tm, LANES), F32),
                        pltpu.VMEM((tm, D), F32)],
        compiler_params=_params(("parallel", "arbitrary")),
        name="proj",
    )(x, g, w_main, w_alr, w_au, b_al, dww, dwb, lng, lnb, ctail0)


def _gla_masks():
    i_ = lax.broadcasted_iota(jnp.int32, (CHUNK, CHUNK), 0)
    j_ = lax.broadcasted_iota(jnp.int32, (CHUNK, CHUNK), 1)
    sh = lambda x, n: jnp.right_shift(x, n)
    one = jnp.ones((CHUNK, CHUNK), F32)
    zero = jnp.zeros((CHUNK, CHUNK), F32)
    sel = lambda m: jnp.where(m, one, zero)
    cum_mat = jnp.concatenate([
        sel(j_ <= i_),
        sel(j_ < 32),
        sel(j_ < jnp.left_shift(sh(i_, 5), 5) + 16),
        sel(j_ < jnp.left_shift(sh(i_, 4), 4) + 8),
        one,
    ], axis=0).astype(BF16)
    up = lambda x, n: jnp.bitwise_and(sh(x, n), 1) == 1
    m32 = sel(up(i_, 5)) * sel(~up(j_, 5))
    m16 = sel(up(i_, 4)) * sel(~up(j_, 4)) * sel(sh(i_, 5) == sh(j_, 5))
    m8 = sel(up(i_, 3)) * sel(~up(j_, 3)) * sel(sh(i_, 4) == sh(j_, 4))
    md = sel(sh(i_, 3) == sh(j_, 3)) * sel(j_ <= i_)
    return cum_mat, (m32 > 0.5, m16 > 0.5, m8 > 0.5, md > 0.5)


def _nt_dot(a, b):
    return lax.dot_general(a, b, (((1,), (1,)), ((), ())), preferred_element_type=F32)


def _gla_head(qf, kf, vb, la, st, cum_mat, masks, nch, row_scr):
    m32, m16, m8, md = masks
    la = la * LOG2_E
    la_hi = la.astype(BF16)
    la_lo = (la - la_hi.astype(F32)).astype(BF16)
    la2 = jnp.concatenate([la_hi, la_lo], axis=1)
    parts = [[] for _ in range(5)]
    for c in range(nch):
        cs = jnp.dot(cum_mat, la2[c * CHUNK:(c + 1) * CHUNK], preferred_element_type=F32)
        cs = cs[:, :GLA_DKH] + cs[:, GLA_DKH:]
        for n in range(5):
            parts[n].append(cs[n * CHUNK:(n + 1) * CHUNK])
    b, p32, p16, p8, bl = (jnp.concatenate(p, axis=0) for p in parts)

    q_in = (qf * jnp.exp2(b)).astype(BF16)
    k_st = (kf * jnp.exp2(bl - b)).astype(BF16)
    q32 = (qf * jnp.exp2(b - p32)).astype(BF16)
    k32 = (kf * jnp.exp2(p32 - b)).astype(BF16)
    q16 = (qf * jnp.exp2(b - p16)).astype(BF16)
    k16 = (kf * jnp.exp2(p16 - b)).astype(BF16)
    q8 = (qf * jnp.exp2(b - p8)).astype(BF16)
    k8 = (kf * jnp.exp2(p8 - b)).astype(BF16)
    col = lax.broadcasted_iota(jnp.int32, (SUB, CHUNK), 1)
    row_scr[0] = kf
    row_scr[1] = b

    outs = []
    for c in range(nch):
        rs_ = slice(c * CHUNK, (c + 1) * CHUNK)
        diag = []
        for g in range(CHUNK // SUB):
            r0 = c * CHUNK + g * SUB
            qb = qf[r0:r0 + SUB]
            bb = b[r0:r0 + SUB]
            a_g = jnp.zeros((SUB, CHUNK), F32)
            for j in range(SUB):
                k_j = row_scr[0, r0 + j:r0 + j + 1, :]
                b_j = row_scr[1, r0 + j:r0 + j + 1, :]
                p = qb * k_j * jnp.exp2(bb - b_j)
                a_g = jnp.where(col == g * SUB + j, jnp.sum(p, axis=1, keepdims=True), a_g)
            diag.append(a_g)
        a = jnp.where(md, jnp.concatenate(diag, axis=0), 0.0)
        a = jnp.where(m8, _nt_dot(q8[rs_], k8[rs_]), a)
        a = jnp.where(m16, _nt_dot(q16[rs_], k16[rs_]), a)
        a = jnp.where(m32, _nt_dot(q32[rs_], k32[rs_]), a)
        v_c = vb[rs_]
        o = jnp.dot(a.astype(BF16), v_c, preferred_element_type=F32) + _nt_dot(q_in[rs_], st.astype(BF16))
        outs.append(o)
        vt = v_c.astype(F32).T.astype(BF16)
        st = st * jnp.exp2(bl[c * CHUNK:c * CHUNK + 1]) + jnp.dot(vt, k_st[rs_], preferred_element_type=F32)
    return jnp.concatenate(outs, axis=0), st


def _mix_kernel(q_ref, k_ref, v_ref, la_ref, rs_ref, cs_ref, sgg_ref, sgc_ref, x_ref,
                gn_ref, s0_ref, wgo_ref, wco_ref, wout_ref, h1_ref, sfin_ref, *scratch, nch):
    st_scrs, row_scrs = scratch[:GLA_HEADS], scratch[GLA_HEADS:]
    t = pl.program_id(1)

    @pl.when(t == 0)
    def _():
        for h in range(GLA_HEADS):
            st_scrs[h][...] = s0_ref[h]

    cum_mat, masks = _gla_masks()
    merged = sgc_ref[0].astype(F32) * jnp.dot(cs_ref[0], wco_ref[...], preferred_element_type=F32)
    parts = []
    for h in range(GLA_HEADS):
        kc = slice(h * GLA_DKH, (h + 1) * GLA_DKH)
        vc = slice(h * GLA_DVH, (h + 1) * GLA_DVH)
        o, st = _gla_head(q_ref[0, :, kc].astype(F32), k_ref[0, :, kc].astype(F32), v_ref[0, :, vc],
                          la_ref[0, :, kc], st_scrs[h][...], cum_mat, masks, nch, row_scrs[h])
        st_scrs[h][...] = st
        ms = jnp.mean(o * o, axis=-1, keepdims=True)
        on = o * lax.rsqrt(ms + RMS_EPS) * gn_ref[h]
        og = (on * rs_ref[0, :, vc].astype(F32)).astype(BF16)
        parts.append(jnp.dot(og, wgo_ref[vc, :], preferred_element_type=F32))

    br_gla = (parts[0] + parts[1]) + (parts[2] + parts[3])
    merged = (merged + sgg_ref[0].astype(F32) * br_gla).astype(BF16)
    h1_ref[0] = x_ref[0] + jnp.dot(merged, wout_ref[...], preferred_element_type=F32)

    @pl.when(t == pl.num_programs(1) - 1)
    def _():
        for h in range(GLA_HEADS):
            sfin_ref[0, h] = st_scrs[h][...]


def _mix(q, k, v, la, rs, cs, sgg, sgc, x, gn, s0, wgo, wco, wout, tb):
    B, S, D = x.shape
    grid = (B, S // tb)
    row = lambda n: pl.BlockSpec((1, tb, n), lambda b, t: (b, t, 0))
    st_shape = (GLA_HEADS, GLA_DVH, GLA_DKH)
    return pl.pallas_call(
        functools.partial(_mix_kernel, nch=tb // CHUNK),
        grid=grid,
        in_specs=[row(GLA_DK), row(GLA_DK), row(GLA_DV), row(GLA_DK), row(GLA_DV), row(D), row(D), row(D), row(D),
                  _const_spec((GLA_HEADS, 1, GLA_DVH)), _const_spec(st_shape),
                  _const_spec((GLA_DV, D)), _const_spec((D, D)), _const_spec((D, D))],
        out_specs=[row(D), pl.BlockSpec((1,) + st_shape, lambda b, t: (b, 0, 0, 0))],
        out_shape=[jax.ShapeDtypeStruct((B, S, D), F32), jax.ShapeDtypeStruct((B,) + st_shape, F32)],
        scratch_shapes=([pltpu.VMEM(st_shape[1:], F32)] * GLA_HEADS
                        + [pltpu.VMEM((2, tb, GLA_DKH), F32)] * GLA_HEADS),
        compiler_params=_params(("parallel", "arbitrary")),
        name="mix",
    )(q, k, v, la, rs, cs, sgg, sgc, x, gn, s0, wgo, wco, wout)


def _ffn_kernel(h1_ref, g_ref, wup_ref, dww_ref, dwb_ref, wdn_ref, gfin_ref, atail0_ref,
                out_ref, atail_ref, win_scr, *, tm):
    i = pl.program_id(1)

    @pl.when(i == 0)
    def _():
        win_scr[0:FFN_HIST, :] = atail0_ref[...]

    h1 = h1_ref[0]
    ms = jnp.mean(h1 * h1, axis=-1, keepdims=True)
    u = (h1 * lax.rsqrt(ms + RMS_EPS) * g_ref[...]).astype(BF16)
    a = jnp.dot(u, wup_ref[:, 0:D_FF], preferred_element_type=F32)
    bv = jnp.dot(u, wup_ref[:, D_FF:2 * D_FF], preferred_element_type=F32)
    win_scr[FFN_HIST:FFN_HIST + tm, :] = a
    off0 = FFN_HIST - (FFN_K - 1)
    ac = dwb_ref[...] + dww_ref[0:1, :] * win_scr[off0:off0 + tm, :]
    for j in range(1, FFN_K):
        ac = ac + dww_ref[j:j + 1, :] * win_scr[off0 + j:off0 + j + tm, :]
    hdn = (ac * _sigmoid(ac) * bv).astype(BF16)
    h2 = h1 + jnp.dot(hdn, wdn_ref[...], preferred_element_type=F32)
    ms2 = jnp.mean(h2 * h2, axis=-1, keepdims=True)
    out_ref[0] = h2 * lax.rsqrt(ms2 + RMS_EPS) * gfin_ref[...]
    tail = win_scr[tm:tm + FFN_HIST, :]
    win_scr[0:FFN_HIST, :] = tail
    atail_ref[0] = tail


def _ffn(h1, g, wup, dww, dwb, wdn, gfin, atail0, tm):
    B, S, D = h1.shape
    grid = (B, S // tm)
    row = pl.BlockSpec((1, tm, D), lambda b, i: (b, i, 0))
    return pl.pallas_call(
        functools.partial(_ffn_kernel, tm=tm),
        grid=grid,
        in_specs=[row, _const_spec((1, D)), _const_spec((D, 2 * D_FF)), _const_spec((FFN_K, D_FF)),
                  _const_spec((1, D_FF)), _const_spec((D_FF, D)), _const_spec((1, D)),
                  _const_spec((FFN_HIST, D_FF))],
        out_specs=[row, pl.BlockSpec((1, FFN_HIST, D_FF), lambda b, i: (b, 0, 0))],
        out_shape=[jax.ShapeDtypeStruct((B, S, D), F32), jax.ShapeDtypeStruct((B, FFN_HIST, D_FF), F32)],
        scratch_shapes=[pltpu.VMEM((FFN_HIST + tm, D_FF), F32)],
        compiler_params=_params(("parallel", "arbitrary")),
        name="ffn",
    )(h1, g, wup, dww, dwb, wdn, gfin, atail0)


def _block(h, carries, wts, tiles, la_mask_rows=0):
    s0, ctail0, atail0 = carries
    tm_proj, tb_mix, tm_ffn = tiles
    q, k, v, rs, la, cs, sgg, sgc, ctail = _proj(
        h, wts["g_mix"], wts["w_main"], wts["w_alr"], wts["w_au"], wts["b_al"],
        wts["dw_w"], wts["dw_b"], wts["ln_g"], wts["ln_b"], ctail0, tm_proj)
    if la_mask_rows:
        la = la.at[:, :la_mask_rows].set(0.0)
    h1, sfin = _mix(q, k, v, la, rs, cs, sgg, sgc, h, wts["g_gla"], s0,
                    wts["w_gla_o"], wts["w_conf_o"], wts["w_out"], tb_mix)
    out, atail = _ffn(h1, wts["g_ffn"], wts["w_up"], wts["ffn_dw_w"], wts["ffn_dw_b"], wts["w_down"],
                      wts["g_fin"], atail0, tm_ffn)
    return out, (sfin[0], ctail[0], atail[0])


def kernel(x, meta_tokens, norm_mix_g, w_in, w_alpha_up, b_alpha, gla_norm_g, w_gla_o, conf_dw_w, conf_dw_b,
           conf_ln_g, conf_ln_b, w_conf_o, w_out, norm_ffn_g, w_up, ffn_dw_w, ffn_dw_b, w_down, final_norm_g):
    assert norm_mix_g.shape[0] == 1, "single-layer block"
    B, S, D = x.shape
    lr0 = 2 * GLA_DK + 2 * GLA_DV
    w_in0 = w_in[0]
    wts = dict(
        g_mix=norm_mix_g[0][None],
        w_main=jnp.concatenate([w_in0[:, :lr0], w_in0[:, lr0 + GLA_RANK:]], axis=1).astype(BF16),
        w_alr=jnp.pad(w_in0[:, lr0:lr0 + GLA_RANK], ((0, 0), (0, RANK_PAD - GLA_RANK))).astype(BF16),
        w_au=jnp.pad(w_alpha_up[0], ((0, RANK_PAD - GLA_RANK), (0, 0))).astype(BF16),
        b_al=b_alpha[0][None],
        g_gla=gla_norm_g[0].reshape(GLA_HEADS, 1, GLA_DVH),
        dw_w=conf_dw_w[0], dw_b=conf_dw_b[0][None], ln_g=conf_ln_g[0][None], ln_b=conf_ln_b[0][None],
        w_gla_o=w_gla_o[0].astype(BF16), w_conf_o=w_conf_o[0].astype(BF16), w_out=w_out[0].astype(BF16),
        g_ffn=norm_ffn_g[0][None], w_up=w_up[0].astype(BF16),
        ffn_dw_w=ffn_dw_w[0], ffn_dw_b=ffn_dw_b[0][None], w_down=w_down[0].astype(BF16),
        g_fin=final_norm_g[None],
    )
    zero_carries = (jnp.zeros((GLA_HEADS, GLA_DVH, GLA_DKH), F32),
                    jnp.zeros((CONV_HIST, D), F32),
                    jnp.zeros((FFN_HIST, D_FF), F32))
    h_meta = jnp.concatenate([jnp.zeros((META_PAD, D), x.dtype), meta_tokens.astype(x.dtype)], axis=0)[None]
    _, carries = _block(h_meta, zero_carries, wts, (CHUNK, CHUNK, CHUNK), la_mask_rows=META_PAD)
    out, _ = _block(x, carries, wts, (512, 256, 256))
    return out
```

```python
import functools

import jax
import jax.numpy as jnp
from jax import lax
from jax.experimental import pallas as pl
from jax.experimental.pallas import tpu as pltpu

F32 = jnp.float32
BF16 = jnp.bfloat16

D_MODEL = 1024
N_META = 16
GLA_HEADS = 4
GLA_DK = 512
GLA_DV = 1024
GLA_DKH = GLA_DK // GLA_HEADS
GLA_DVH = GLA_DV // GLA_HEADS
GLA_RANK = 16
GATE_TAU = 16.0
LOG2_E = 1.4426950408889634
CHUNK = 64
SUB = 8
META_PAD = CHUNK - N_META
CONF_K = 31
CONV_HIST = 32
CONV_RB = 64
D_FF = 2816
FFN_K = 3
FFN_HIST = 8
RMS_EPS = 1e-6
LN_EPS = 1e-5
LANES = 128
SUBLANES = 8
RANK_PAD = LANES
VMEM_LIMIT = 60 * 1024 * 1024
SEQ_TILE = 512

_COLS = dict(q=(0, 0, 512), k=(0, 512, 1024), v=(0, 1024, 2048), r=(0, 2048, 3072),
             c1=(1, 0, 1024), c2=(1, 1024, 2048), gg=(1, 2048, 3072), gc=(1, 3072, 4096))
N_GROUP = (3072, 4096)


def _sigmoid(x):
    return 1.0 / (1.0 + jnp.exp(-x))


def _log_sigmoid(x):
    return jnp.minimum(x, 0.0) - jnp.log(1.0 + jnp.exp(-jnp.abs(x)))


def _const_spec(shape):
    nd = len(shape)
    return pl.BlockSpec(shape, lambda *_: (0,) * nd, pipeline_mode=pl.Buffered(1))


def _params(sem):
    return pltpu.CompilerParams(dimension_semantics=sem, vmem_limit_bytes=VMEM_LIMIT)


def _zero_after(v):
    bits = lax.bitcast_convert_type(v, jnp.uint32)
    half_word = jnp.uint32(16)
    bits = lax.shift_right_logical(lax.shift_right_logical(bits, half_word), half_word)
    return lax.bitcast_convert_type(bits, F32)


def _conv31_block(cb, win_scr, sh_scr, dww_ref, dwb_ref, conv_scr, tm, after):
    nw = CONV_HIST + tm
    off0 = CONV_HIST - (CONF_K - 1)
    cols = slice(cb * LANES, (cb + 1) * LANES)
    sh = sh_scr.at[cb % 2]
    wblk = win_scr[:, cols]
    for r in range(1, SUBLANES):
        sh[r - 1] = pltpu.roll(wblk, nw - r, 0)
    bias = jnp.broadcast_to(dwb_ref[:, cols], (SUBLANES, LANES)) + _zero_after(after)
    for rb in range(tm // CONV_RB):
        acc = jnp.concatenate([bias] * (CONV_RB // SUBLANES), axis=0)
        for j in range(CONF_K):
            r = (off0 + j) % SUBLANES
            start = rb * CONV_RB + (off0 + j) - r
            if r == 0:
                src = win_scr[start:start + CONV_RB, cols]
            else:
                src = sh[r - 1, start:start + CONV_RB, :]
            acc = acc + dww_ref[j:j + 1, cols] * src
        conv_scr[rb * CONV_RB:(rb + 1) * CONV_RB, cols] = acc
    return acc[CONV_RB - SUBLANES:CONV_RB]


def _proj_kernel(x_ref, g_ref, wa_ref, wb_ref, walr_ref, wau_ref, bal_ref, dww_ref, dwb_ref, lng_ref, lnb_ref,
                 ctail0_ref,
                 q_ref, k_ref, v_ref, rs_ref, la_ref, cs_ref, sgg_ref, sgc_ref, ctail_ref,
                 win_scr, sh_scr, conv_scr, *, tm):
    i = pl.program_id(1)

    @pl.when(i == 0)
    def _():
        win_scr[0:CONV_HIST, :] = ctail0_ref[...]

    x = x_ref[0]
    ms = jnp.mean(x * x, axis=-1, keepdims=True)
    u = (x * lax.rsqrt(ms + RMS_EPS) * g_ref[...]).astype(BF16)

    def mm(name, part=0, nparts=1):
        group, lo, hi = _COLS[name]
        w_ref = (wa_ref, wb_ref)[group]
        w = (hi - lo) // nparts
        return jnp.dot(u, w_ref[:, lo + part * w:lo + (part + 1) * w], preferred_element_type=F32)

    def half(n, part):
        return slice(part * (n // 2), (part + 1) * (n // 2))

    def last_tile(res, after):
        tile = res[tm - SUBLANES:tm, 0:LANES]
        return tile if after is None else tile + _zero_after(after)

    n_glu = 4
    glu_w = D_MODEL // n_glu
    n_blocks = D_MODEL // LANES
    n_norm = 4
    norm_rows = tm // n_norm

    def p_glu(part, after):
        c1 = mm("c1", part, n_glu)
        c2 = mm("c2", part, n_glu)
        win_scr[CONV_HIST:CONV_HIST + tm, part * glu_w:(part + 1) * glu_w] = c1 * _sigmoid(c2)
        return last_tile(c2, after)

    def p_q(after):
        res = mm("q")
        q_ref[0] = (res * (GLA_DKH ** -0.5)).astype(BF16)
        return last_tile(res, after)

    def p_k(after):
        res = mm("k")
        k_ref[0] = res.astype(BF16)
        return last_tile(res, after)

    def p_v(part, after):
        res = mm("v", part, 2)
        v_ref[0, :, half(GLA_DV, part)] = res.astype(BF16)
        return last_tile(res, after)

    def p_r(part, after):
        r = mm("r", part, 2)
        rs_ref[0, :, half(GLA_DV, part)] = (r * _sigmoid(r)).astype(BF16)
        return last_tile(r, after)

    def p_gate(name, ref, part, after):
        res = mm(name, part, 2)
        ref[0, :, half(D_MODEL, part)] = _sigmoid(res).astype(BF16)
        return last_tile(res, after)

    def p_la(after):
        alr = jnp.dot(u, walr_ref[...], preferred_element_type=F32)
        xa = jnp.dot(alr.astype(BF16), wau_ref[...], preferred_element_type=F32) + bal_ref[...]
        la_ref[0] = _log_sigmoid(xa) * (1.0 / GATE_TAU)
        return last_tile(xa, after)

    def u_conv(cb, after):
        return _conv31_block(cb, win_scr, sh_scr, dww_ref, dwb_ref, conv_scr, tm, after)

    def u_norm(part, after):
        rows = slice(part * norm_rows, (part + 1) * norm_rows)
        zero_row = jnp.concatenate([_zero_after(after)[0:1]] * n_blocks, axis=1)
        cv = conv_scr[rows, :] + zero_row
        mu = jnp.mean(cv, axis=-1, keepdims=True)
        xc = cv - mu
        var = jnp.mean(xc * xc, axis=-1, keepdims=True)
        y = xc * lax.rsqrt(var + LN_EPS) * lng_ref[...] + lnb_ref[...]
        cs_ref[0, rows, :] = (y * _sigmoid(y)).astype(BF16)
        return y[norm_rows - SUBLANES:norm_rows, 0:LANES]

    P = functools.partial
    pieces = [P(p_glu, 0), P(p_glu, 1), p_la, P(p_glu, 2), P(p_glu, 3), p_q, p_k, P(p_v, 0), P(p_v, 1),
              P(p_r, 0), P(p_r, 1), P(p_gate, "gg", sgg_ref, 0), P(p_gate, "gg", sgg_ref, 1),
              P(p_gate, "gc", sgc_ref, 0), P(p_gate, "gc", sgc_ref, 1)]
    units = [P(u_conv, cb) for cb in range(n_blocks)] + [P(u_norm, part) for part in range(n_norm)]
    groups = [[0], [1], [2, 3], [4], [5, 6], [7], [8, 9], [10], [11], [12], [13], [14]]
    folds = {groups[j + 2][-1]: j for j in range(len(units) - 2)}
    unit_tiles = []
    for j, unit in enumerate(units):
        tiles = [pieces[m](unit_tiles[folds[m]] if m in folds else None) for m in groups[j]]
        unit_tiles.append(unit(functools.reduce(lambda a, b: a + b, tiles)))
        if j == n_blocks - 1:
            tail = win_scr[tm:tm + CONV_HIST, :]
            win_scr[0:CONV_HIST, :] = tail
            ctail_ref[0] = tail


def _proj(x, g, w_a, w_b, w_alr, w_au, b_al, dww, dwb, lng, lnb, ctail0, tm):
    B, S, D = x.shape
    grid = (B, S // tm)
    row = lambda n: pl.BlockSpec((1, tm, n), lambda b, i: (b, i, 0))
    bf = lambda n: jax.ShapeDtypeStruct((B, S, n), BF16)
    return pl.pallas_call(
        functools.partial(_proj_kernel, tm=tm),
        grid=grid,
        in_specs=[row(D), _const_spec((1, D)), _const_spec((D, N_GROUP[0])), _const_spec((D, N_GROUP[1])),
                  _const_spec((D, RANK_PAD)),
                  _const_spec((RANK_PAD, GLA_DK)), _const_spec((1, GLA_DK)),
                  _const_spec((CONF_K, D)), _const_spec((1, D)), _const_spec((1, D)), _const_spec((1, D)),
                  _const_spec((CONV_HIST, D))],
        out_specs=[row(GLA_DK), row(GLA_DK), row(GLA_DV), row(GLA_DV), row(GLA_DK), row(D), row(D), row(D),
                   pl.BlockSpec((1, CONV_HIST, D), lambda b, i: (b, 0, 0))],
        out_shape=[bf(GLA_DK), bf(GLA_DK), bf(GLA_DV), bf(GLA_DV),
                   jax.ShapeDtypeStruct((B, S, GLA_DK), F32), bf(D), bf(D), bf(D),
                   jax.ShapeDtypeStruct((B, CONV_HIST, D), F32)],
        scratch_shapes=[pltpu.VMEM((CONV_HIST + tm, D), F32),
                        pltpu.VMEM((2, SUBLANES - 1, CONV_HIST + tm, LANES), F32),
                        pltpu.VMEM((tm, D), F32)],
        compiler_params=_params(("parallel", "arbitrary")),
        name="proj",
    )(x, g, w_a, w_b, w_alr, w_au, b_al, dww, dwb, lng, lnb, ctail0)


def _gla_masks():
    i_ = lax.broadcasted_iota(jnp.int32, (CHUNK, CHUNK), 0)
    j_ = lax.broadcasted_iota(jnp.int32, (CHUNK, CHUNK), 1)
    sh = lambda x, n: jnp.right_shift(x, n)
    one = jnp.ones((CHUNK, CHUNK), F32)
    zero = jnp.zeros((CHUNK, CHUNK), F32)
    sel = lambda m: jnp.where(m, one, zero)
    cum_mat = jnp.concatenate([
        sel(j_ <= i_),
        sel(j_ < 32),
        sel(j_ < jnp.left_shift(sh(i_, 5), 5) + 16),
        sel(j_ < jnp.left_shift(sh(i_, 4), 4) + 8),
        one,
    ], axis=0).astype(BF16)
    up = lambda x, n: jnp.bitwise_and(sh(x, n), 1) == 1
    m32 = sel(up(i_, 5)) * sel(~up(j_, 5))
    m16 = sel(up(i_, 4)) * sel(~up(j_, 4)) * sel(sh(i_, 5) == sh(j_, 5))
    m8 = sel(up(i_, 3)) * sel(~up(j_, 3)) * sel(sh(i_, 4) == sh(j_, 4))
    md = sel(sh(i_, 3) == sh(j_, 3)) * sel(j_ <= i_)
    return cum_mat, (m32 > 0.5, m16 > 0.5, m8 > 0.5, md > 0.5)


def _nt_dot(a, b):
    return lax.dot_general(a, b, (((1,), (1,)), ((), ())), preferred_element_type=F32)


def _gla_head(qf, kf, vb, la, st, cum_mat, masks, nch, row_scr):
    m32, m16, m8, md = masks
    la = la * LOG2_E
    la_hi = la.astype(BF16)
    la_lo = (la - la_hi.astype(F32)).astype(BF16)
    la2 = jnp.concatenate([la_hi, la_lo], axis=1)
    parts = [[] for _ in range(5)]
    for c in range(nch):
        cs = jnp.dot(cum_mat, la2[c * CHUNK:(c + 1) * CHUNK], preferred_element_type=F32)
        cs = cs[:, :GLA_DKH] + cs[:, GLA_DKH:]
        for n in range(5):
            parts[n].append(cs[n * CHUNK:(n + 1) * CHUNK])
    b, p32, p16, p8, bl = (jnp.concatenate(p, axis=0) for p in parts)

    q_in = (qf * jnp.exp2(b)).astype(BF16)
    k_st = (kf * jnp.exp2(bl - b)).astype(BF16)
    q32 = (qf * jnp.exp2(b - p32)).astype(BF16)
    k32 = (kf * jnp.exp2(p32 - b)).astype(BF16)
    q16 = (qf * jnp.exp2(b - p16)).astype(BF16)
    k16 = (kf * jnp.exp2(p16 - b)).astype(BF16)
    q8 = (qf * jnp.exp2(b - p8)).astype(BF16)
    k8 = (kf * jnp.exp2(p8 - b)).astype(BF16)
    col = lax.broadcasted_iota(jnp.int32, (SUB, CHUNK), 1)
    row_scr[0] = kf
    row_scr[1] = b

    outs = []
    for c in range(nch):
        rs_ = slice(c * CHUNK, (c + 1) * CHUNK)
        diag = []
        for g in range(CHUNK // SUB):
            r0 = c * CHUNK + g * SUB
            qb = qf[r0:r0 + SUB]
            bb = b[r0:r0 + SUB]
            a_g = jnp.zeros((SUB, CHUNK), F32)
            for j in range(SUB):
                k_j = row_scr[0, r0 + j:r0 + j + 1, :]
                b_j = row_scr[1, r0 + j:r0 + j + 1, :]
                p = qb * k_j * jnp.exp2(bb - b_j)
                a_g = jnp.where(col == g * SUB + j, jnp.sum(p, axis=1, keepdims=True), a_g)
            diag.append(a_g)
        a = jnp.where(md, jnp.concatenate(diag, axis=0), 0.0)
        a = jnp.where(m8, _nt_dot(q8[rs_], k8[rs_]), a)
        a = jnp.where(m16, _nt_dot(q16[rs_], k16[rs_]), a)
        a = jnp.where(m32, _nt_dot(q32[rs_], k32[rs_]), a)
        v_c = vb[rs_]
        o = jnp.dot(a.astype(BF16), v_c, preferred_element_type=F32) + _nt_dot(q_in[rs_], st.astype(BF16))
        outs.append(o)
        upd = lax.dot_general(v_c, k_st[rs_], (((0,), (0,)), ((), ())), preferred_element_type=F32)
        st = st * jnp.exp2(bl[c * CHUNK:c * CHUNK + 1]) + upd
    return jnp.concatenate(outs, axis=0), st


def _mix_kernel(q_ref, k_ref, v_ref, la_ref, rs_ref, cs_ref, sgg_ref, sgc_ref, x_ref,
                gn_ref, s0_ref, wgo_ref, wco_ref, wout_ref, h1_ref, sfin_ref, *scratch, nch):
    st_scrs, row_scrs = scratch[:GLA_HEADS], scratch[GLA_HEADS:]
    t = pl.program_id(1)

    @pl.when(t == 0)
    def _():
        for h in range(GLA_HEADS):
            st_scrs[h][...] = s0_ref[h]

    cum_mat, masks = _gla_masks()
    merged = sgc_ref[0].astype(F32) * jnp.dot(cs_ref[0], wco_ref[...], preferred_element_type=F32)
    parts = []
    for h in range(GLA_HEADS):
        kc = slice(h * GLA_DKH, (h + 1) * GLA_DKH)
        vc = slice(h * GLA_DVH, (h + 1) * GLA_DVH)
        o, st = _gla_head(q_ref[0, :, kc].astype(F32), k_ref[0, :, kc].astype(F32), v_ref[0, :, vc],
                          la_ref[0, :, kc], st_scrs[h][...], cum_mat, masks, nch, row_scrs[h])
        st_scrs[h][...] = st
        ms = jnp.mean(o * o, axis=-1, keepdims=True)
        on = o * lax.rsqrt(ms + RMS_EPS) * gn_ref[h]
        og = (on * rs_ref[0, :, vc].astype(F32)).astype(BF16)
        parts.append(jnp.dot(og, wgo_ref[vc, :], preferred_element_type=F32))

    br_gla = (parts[0] + parts[1]) + (parts[2] + parts[3])
    merged = (merged + sgg_ref[0].astype(F32) * br_gla).astype(BF16)
    h1_ref[0] = x_ref[0] + jnp.dot(merged, wout_ref[...], preferred_element_type=F32)

    @pl.when(t == pl.num_programs(1) - 1)
    def _():
        for h in range(GLA_HEADS):
            sfin_ref[0, h] = st_scrs[h][...]


def _mix(q, k, v, la, rs, cs, sgg, sgc, x, gn, s0, wgo, wco, wout, tb):
    B, S, D = x.shape
    grid = (B, S // tb)
    row = lambda n: pl.BlockSpec((1, tb, n), lambda b, t: (b, t, 0))
    st_shape = (GLA_HEADS, GLA_DVH, GLA_DKH)
    return pl.pallas_call(
        functools.partial(_mix_kernel, nch=tb // CHUNK),
        grid=grid,
        in_specs=[row(GLA_DK), row(GLA_DK), row(GLA_DV), row(GLA_DK), row(GLA_DV), row(D), row(D), row(D), row(D),
                  _const_spec((GLA_HEADS, 1, GLA_DVH)), _const_spec(st_shape),
                  _const_spec((GLA_DV, D)), _const_spec((D, D)), _const_spec((D, D))],
        out_specs=[row(D), pl.BlockSpec((1,) + st_shape, lambda b, t: (b, 0, 0, 0))],
        out_shape=[jax.ShapeDtypeStruct((B, S, D), F32), jax.ShapeDtypeStruct((B,) + st_shape, F32)],
        scratch_shapes=([pltpu.VMEM(st_shape[1:], F32)] * GLA_HEADS
                        + [pltpu.VMEM((2, tb, GLA_DKH), F32)] * GLA_HEADS),
        compiler_params=_params(("parallel", "arbitrary")),
        name="mix",
    )(q, k, v, la, rs, cs, sgg, sgc, x, gn, s0, wgo, wco, wout)


def _ffn_kernel(h1_ref, g_ref, wup_ref, dww_ref, dwb_ref, wdn_ref, gfin_ref, atail0_ref,
                out_ref, atail_ref, win_scr, *, tm):
    i = pl.program_id(1)

    @pl.when(i == 0)
    def _():
        win_scr[0:FFN_HIST, :] = atail0_ref[...]

    h1 = h1_ref[0]
    ms = jnp.mean(h1 * h1, axis=-1, keepdims=True)
    u = (h1 * lax.rsqrt(ms + RMS_EPS) * g_ref[...]).astype(BF16)
    a = jnp.dot(u, wup_ref[:, 0:D_FF], preferred_element_type=F32)
    bv = jnp.dot(u, wup_ref[:, D_FF:2 * D_FF], preferred_element_type=F32)
    win_scr[FFN_HIST:FFN_HIST + tm, :] = a
    off0 = FFN_HIST - (FFN_K - 1)
    ac = dwb_ref[...] + dww_ref[0:1, :] * win_scr[off0:off0 + tm, :]
    for j in range(1, FFN_K):
        ac = ac + dww_ref[j:j + 1, :] * win_scr[off0 + j:off0 + j + tm, :]
    hdn = (ac * _sigmoid(ac) * bv).astype(BF16)
    h2 = h1 + jnp.dot(hdn, wdn_ref[...], preferred_element_type=F32)
    ms2 = jnp.mean(h2 * h2, axis=-1, keepdims=True)
    out_ref[0] = h2 * lax.rsqrt(ms2 + RMS_EPS) * gfin_ref[...]
    tail = win_scr[tm:tm + FFN_HIST, :]
    win_scr[0:FFN_HIST, :] = tail
    atail_ref[0] = tail


def _ffn(h1, g, wup, dww, dwb, wdn, gfin, atail0, tm):
    B, S, D = h1.shape
    grid = (B, S // tm)
    row = pl.BlockSpec((1, tm, D), lambda b, i: (b, i, 0))
    return pl.pallas_call(
        functools.partial(_ffn_kernel, tm=tm),
        grid=grid,
        in_specs=[row, _const_spec((1, D)), _const_spec((D, 2 * D_FF)), _const_spec((FFN_K, D_FF)),
                  _const_spec((1, D_FF)), _const_spec((D_FF, D)), _const_spec((1, D)),
                  _const_spec((FFN_HIST, D_FF))],
        out_specs=[row, pl.BlockSpec((1, FFN_HIST, D_FF), lambda b, i: (b, 0, 0))],
        out_shape=[jax.ShapeDtypeStruct((B, S, D), F32), jax.ShapeDtypeStruct((B, FFN_HIST, D_FF), F32)],
        scratch_shapes=[pltpu.VMEM((FFN_HIST + tm, D_FF), F32)],
        compiler_params=_params(("parallel", "arbitrary")),
        name="ffn",
    )(h1, g, wup, dww, dwb, wdn, gfin, atail0)


def _block(h, carries, wts, tiles, la_mask_rows=0):
    s0, ctail0, atail0 = carries
    tm_proj, tb_mix, tm_ffn = tiles
    q, k, v, rs, la, cs, sgg, sgc, ctail = _proj(
        h, wts["g_mix"], wts["w_a"], wts["w_b"], wts["w_alr"], wts["w_au"], wts["b_al"],
        wts["dw_w"], wts["dw_b"], wts["ln_g"], wts["ln_b"], ctail0, tm_proj)
    if la_mask_rows:
        la = la.at[:, :la_mask_rows].set(0.0)
    h1, sfin = _mix(q, k, v, la, rs, cs, sgg, sgc, h, wts["g_gla"], s0,
                    wts["w_gla_o"], wts["w_conf_o"], wts["w_out"], tb_mix)
    out, atail = _ffn(h1, wts["g_ffn"], wts["w_up"], wts["ffn_dw_w"], wts["ffn_dw_b"], wts["w_down"],
                      wts["g_fin"], atail0, tm_ffn)
    return out, (sfin[0], ctail[0], atail[0])


def kernel(x, meta_tokens, norm_mix_g, w_in, w_alpha_up, b_alpha, gla_norm_g, w_gla_o, conf_dw_w, conf_dw_b,
           conf_ln_g, conf_ln_b, w_conf_o, w_out, norm_ffn_g, w_up, ffn_dw_w, ffn_dw_b, w_down, final_norm_g):
    assert norm_mix_g.shape[0] == 1, "single-layer block"
    B, S, D = x.shape
    lr0 = 2 * GLA_DK + 2 * GLA_DV
    w_in0 = w_in[0]
    wts = dict(
        g_mix=norm_mix_g[0][None],
        w_a=w_in0[:, :lr0].astype(BF16), w_b=w_in0[:, lr0 + GLA_RANK:].astype(BF16),
        w_alr=jnp.pad(w_in0[:, lr0:lr0 + GLA_RANK], ((0, 0), (0, RANK_PAD - GLA_RANK))).astype(BF16),
        w_au=jnp.pad(w_alpha_up[0], ((0, RANK_PAD - GLA_RANK), (0, 0))).astype(BF16),
        b_al=b_alpha[0][None],
        g_gla=gla_norm_g[0].reshape(GLA_HEADS, 1, GLA_DVH),
        dw_w=conf_dw_w[0], dw_b=conf_dw_b[0][None], ln_g=conf_ln_g[0][None], ln_b=conf_ln_b[0][None],
        w_gla_o=w_gla_o[0].astype(BF16), w_conf_o=w_conf_o[0].astype(BF16), w_out=w_out[0].astype(BF16),
        g_ffn=norm_ffn_g[0][None], w_up=w_up[0].astype(BF16),
        ffn_dw_w=ffn_dw_w[0], ffn_dw_b=ffn_dw_b[0][None], w_down=w_down[0].astype(BF16),
        g_fin=final_norm_g[None],
    )
    zero_carries = (jnp.zeros((GLA_HEADS, GLA_DVH, GLA_DKH), F32),
                    jnp.zeros((CONV_HIST, D), F32),
                    jnp.zeros((FFN_HIST, D_FF), F32))
    h_meta = jnp.concatenate([jnp.zeros((META_PAD, D), x.dtype), meta_tokens.astype(x.dtype)], axis=0)[None]
    _, carries = _block(h_meta, zero_carries, wts, (CHUNK, CHUNK, CHUNK), la_mask_rows=META_PAD)
    out, _ = _block(x, carries, wts, (SEQ_TILE,) * 3)
    return out
```

```python
import functools

import jax
import jax.numpy as jnp
from jax import lax
from jax.experimental import pallas as pl
from jax.experimental.pallas import tpu as pltpu

F32 = jnp.float32
BF16 = jnp.bfloat16

D_MODEL = 1024
N_META = 16
GLA_HEADS = 4
GLA_DK = 512
GLA_DV = 1024
GLA_DKH = GLA_DK // GLA_HEADS
GLA_DVH = GLA_DV // GLA_HEADS
GLA_RANK = 16
GATE_TAU = 16.0
LOG2_E = 1.4426950408889634
CHUNK = 64
SUB = 8
META_PAD = CHUNK - N_META
CONF_K = 31
CONV_HIST = 32
CONV_RB = 64
D_FF = 2816
FFN_K = 3
FFN_HIST = 8
RMS_EPS = 1e-6
LN_EPS = 1e-5
LANES = 128
SUBLANES = 8
RANK_PAD = LANES
VMEM_LIMIT = 60 * 1024 * 1024
SEQ_TILE = 512

_COLS = dict(q=(0, 0, 512), k=(0, 512, 1024), v=(0, 1024, 2048), r=(0, 2048, 3072),
             c1=(1, 0, 1024), c2=(1, 1024, 2048), gg=(1, 2048, 3072), gc=(1, 3072, 4096))
N_GROUP = (3072, 4096)


def _sigmoid(x):
    return 1.0 / (1.0 + jnp.exp(-x))


def _log_sigmoid(x):
    return jnp.minimum(x, 0.0) - jnp.log(1.0 + jnp.exp(-jnp.abs(x)))


def _const_spec(shape):
    nd = len(shape)
    return pl.BlockSpec(shape, lambda *_: (0,) * nd, pipeline_mode=pl.Buffered(1))


def _params(sem):
    return pltpu.CompilerParams(dimension_semantics=sem, vmem_limit_bytes=VMEM_LIMIT)


def _zero_after(v):
    bits = lax.bitcast_convert_type(v, jnp.uint32)
    half_word = jnp.uint32(16)
    bits = lax.shift_right_logical(lax.shift_right_logical(bits, half_word), half_word)
    return lax.bitcast_convert_type(bits, F32)


def _conv31_block(cb, win_scr, sh_scr, dww_ref, dwb_ref, conv_scr, tm, after):
    nw = CONV_HIST + tm
    off0 = CONV_HIST - (CONF_K - 1)
    cols = slice(cb * LANES, (cb + 1) * LANES)
    sh = sh_scr.at[cb % 2]
    wblk = win_scr[:, cols]
    for r in range(1, SUBLANES):
        sh[r - 1] = pltpu.roll(wblk, nw - r, 0)
    bias = jnp.broadcast_to(dwb_ref[:, cols], (SUBLANES, LANES)) + _zero_after(after)
    for rb in range(tm // CONV_RB):
        acc = jnp.concatenate([bias] * (CONV_RB // SUBLANES), axis=0)
        for j in range(CONF_K):
            r = (off0 + j) % SUBLANES
            start = rb * CONV_RB + (off0 + j) - r
            if r == 0:
                src = win_scr[start:start + CONV_RB, cols]
            else:
                src = sh[r - 1, start:start + CONV_RB, :]
            acc = acc + dww_ref[j:j + 1, cols] * src
        conv_scr[rb * CONV_RB:(rb + 1) * CONV_RB, cols] = acc
    return acc[CONV_RB - SUBLANES:CONV_RB]


def _proj_kernel(x_ref, g_ref, wa_ref, wb_ref, walr_ref, wau_ref, bal_ref, dww_ref, dwb_ref, lng_ref, lnb_ref,
                 ctail0_ref,
                 q_ref, k_ref, v_ref, rs_ref, la_ref, cs_ref, sgg_ref, sgc_ref, ctail_ref,
                 win_scr, sh_scr, conv_scr, *, tm):
    i = pl.program_id(1)

    @pl.when(i == 0)
    def _():
        win_scr[0:CONV_HIST, :] = ctail0_ref[...]

    x = x_ref[0]
    ms = jnp.mean(x * x, axis=-1, keepdims=True)
    u = (x * lax.rsqrt(ms + RMS_EPS) * g_ref[...]).astype(BF16)

    def mm(name, part=0, nparts=1):
        group, lo, hi = _COLS[name]
        w_ref = (wa_ref, wb_ref)[group]
        w = (hi - lo) // nparts
        return jnp.dot(u, w_ref[:, lo + part * w:lo + (part + 1) * w], preferred_element_type=F32)

    def half(n, part):
        return slice(part * (n // 2), (part + 1) * (n // 2))

    def last_tile(res, after):
        tile = res[tm - SUBLANES:tm, 0:LANES]
        return tile if after is None else tile + _zero_after(after)

    n_glu = 4
    glu_w = D_MODEL // n_glu
    n_blocks = D_MODEL // LANES
    n_norm = 4
    norm_rows = tm // n_norm

    def p_glu(part, after):
        c1 = mm("c1", part, n_glu)
        c2 = mm("c2", part, n_glu)
        win_scr[CONV_HIST:CONV_HIST + tm, part * glu_w:(part + 1) * glu_w] = c1 * _sigmoid(c2)
        return last_tile(c2, after)

    def p_q(after):
        res = mm("q")
        q_ref[0] = (res * (GLA_DKH ** -0.5)).astype(BF16)
        return last_tile(res, after)

    def p_k(after):
        res = mm("k")
        k_ref[0] = res.astype(BF16)
        return last_tile(res, after)

    def p_v(part, after):
        res = mm("v", part, 2)
        v_ref[0, :, half(GLA_DV, part)] = res.astype(BF16)
        return last_tile(res, after)

    def p_r(part, after):
        r = mm("r", part, 2)
        rs_ref[0, :, half(GLA_DV, part)] = (r * _sigmoid(r)).astype(BF16)
        return last_tile(r, after)

    def p_gate(name, ref, part, after):
        res = mm(name, part, 2)
        ref[0, :, half(D_MODEL, part)] = _sigmoid(res).astype(BF16)
        return last_tile(res, after)

    def p_la(after):
        alr = jnp.dot(u, walr_ref[...], preferred_element_type=F32)
        xa = jnp.dot(alr.astype(BF16), wau_ref[...], preferred_element_type=F32) + bal_ref[...]
        la_ref[0] = _log_sigmoid(xa) * (1.0 / GATE_TAU)
        return last_tile(xa, after)

    def u_conv(cb, after):
        return _conv31_block(cb, win_scr, sh_scr, dww_ref, dwb_ref, conv_scr, tm, after)

    def u_norm(part, after):
        rows = slice(part * norm_rows, (part + 1) * norm_rows)
        zero_row = jnp.concatenate([_zero_after(after)[0:1]] * n_blocks, axis=1)
        cv = conv_scr[rows, :] + zero_row
        mu = jnp.mean(cv, axis=-1, keepdims=True)
        xc = cv - mu
        var = jnp.mean(xc * xc, axis=-1, keepdims=True)
        y = xc * lax.rsqrt(var + LN_EPS) * lng_ref[...] + lnb_ref[...]
        cs_ref[0, rows, :] = (y * _sigmoid(y)).astype(BF16)
        return y[norm_rows - SUBLANES:norm_rows, 0:LANES]

    P = functools.partial
    pieces = [P(p_glu, 0), P(p_glu, 1), p_la, P(p_glu, 2), P(p_glu, 3), p_q, p_k, P(p_v, 0), P(p_v, 1),
              P(p_r, 0), P(p_r, 1), P(p_gate, "gg", sgg_ref, 0), P(p_gate, "gg", sgg_ref, 1),
              P(p_gate, "gc", sgc_ref, 0), P(p_gate, "gc", sgc_ref, 1)]
    units = [P(u_conv, cb) for cb in range(n_blocks)] + [P(u_norm, part) for part in range(n_norm)]
    groups = [[0], [1], [2, 3], [4], [5, 6], [7], [8, 9], [10], [11], [12], [13], [14]]
    folds = {groups[j + 1][-1]: j for j in range(len(units) - 1)}
    unit_tiles = []
    for j, unit in enumerate(units):
        tiles = [pieces[m](unit_tiles[folds[m]] if m in folds else None) for m in groups[j]]
        unit_tiles.append(unit(functools.reduce(lambda a, b: a + b, tiles)))
        if j == n_blocks - 1:
            tail = win_scr[tm:tm + CONV_HIST, :]
            win_scr[0:CONV_HIST, :] = tail
            ctail_ref[0] = tail


def _proj(x, g, w_a, w_b, w_alr, w_au, b_al, dww, dwb, lng, lnb, ctail0, tm):
    B, S, D = x.shape
    grid = (B, S // tm)
    row = lambda n: pl.BlockSpec((1, tm, n), lambda b, i: (b, i, 0))
    bf = lambda n: jax.ShapeDtypeStruct((B, S, n), BF16)
    return pl.pallas_call(
        functools.partial(_proj_kernel, tm=tm),
        grid=grid,
        in_specs=[row(D), _const_spec((1, D)), _const_spec((D, N_GROUP[0])), _const_spec((D, N_GROUP[1])),
                  _const_spec((D, RANK_PAD)),
                  _const_spec((RANK_PAD, GLA_DK)), _const_spec((1, GLA_DK)),
                  _const_spec((CONF_K, D)), _const_spec((1, D)), _const_spec((1, D)), _const_spec((1, D)),
                  _const_spec((CONV_HIST, D))],
        out_specs=[row(GLA_DK), row(GLA_DK), row(GLA_DV), row(GLA_DV), row(GLA_DK), row(D), row(D), row(D),
                   pl.BlockSpec((1, CONV_HIST, D), lambda b, i: (b, 0, 0))],
        out_shape=[bf(GLA_DK), bf(GLA_DK), bf(GLA_DV), bf(GLA_DV),
                   jax.ShapeDtypeStruct((B, S, GLA_DK), F32), bf(D), bf(D), bf(D),
                   jax.ShapeDtypeStruct((B, CONV_HIST, D), F32)],
        scratch_shapes=[pltpu.VMEM((CONV_HIST + tm, D), F32),
                        pltpu.VMEM((2, SUBLANES - 1, CONV_HIST + tm, LANES), F32),
                        pltpu.VMEM((tm, D), F32)],
        compiler_params=_params(("parallel", "arbitrary")),
        name="proj",
    )(x, g, w_a, w_b, w_alr, w_au, b_al, dww, dwb, lng, lnb, ctail0)


def _gla_masks():
    i_ = lax.broadcasted_iota(jnp.int32, (CHUNK, CHUNK), 0)
    j_ = lax.broadcasted_iota(jnp.int32, (CHUNK, CHUNK), 1)
    sh = lambda x, n: jnp.right_shift(x, n)
    one = jnp.ones((CHUNK, CHUNK), F32)
    zero = jnp.zeros((CHUNK, CHUNK), F32)
    sel = lambda m: jnp.where(m, one, zero)
    cum_mat = jnp.concatenate([
        sel(j_ <= i_),
        sel(j_ < 32),
        sel(j_ < jnp.left_shift(sh(i_, 5), 5) + 16),
        sel(j_ < jnp.left_shift(sh(i_, 4), 4) + 8),
        one,
    ], axis=0).astype(BF16)
    up = lambda x, n: jnp.bitwise_and(sh(x, n), 1) == 1
    m32 = sel(up(i_, 5)) * sel(~up(j_, 5))
    m16 = sel(up(i_, 4)) * sel(~up(j_, 4)) * sel(sh(i_, 5) == sh(j_, 5))
    m8 = sel(up(i_, 3)) * sel(~up(j_, 3)) * sel(sh(i_, 4) == sh(j_, 4))
    md = sel(sh(i_, 3) == sh(j_, 3)) * sel(j_ <= i_)
    return cum_mat, (m32 > 0.5, m16 > 0.5, m8 > 0.5, md > 0.5)


def _nt_dot(a, b):
    return lax.dot_general(a, b, (((1,), (1,)), ((), ())), preferred_element_type=F32)


def _gla_head(qf, kf, vb, la, st, cum_mat, masks, nch, row_scr):
    m32, m16, m8, md = masks
    la = la * LOG2_E
    la_hi = la.astype(BF16)
    la_lo = (la - la_hi.astype(F32)).astype(BF16)
    la2 = jnp.concatenate([la_hi, la_lo], axis=1)
    parts = [[] for _ in range(5)]
    for c in range(nch):
        cs = jnp.dot(cum_mat, la2[c * CHUNK:(c + 1) * CHUNK], preferred_element_type=F32)
        cs = cs[:, :GLA_DKH] + cs[:, GLA_DKH:]
        for n in range(5):
            parts[n].append(cs[n * CHUNK:(n + 1) * CHUNK])
    b, p32, p16, p8, bl = (jnp.concatenate(p, axis=0) for p in parts)

    q_in = (qf * jnp.exp2(b)).astype(BF16)
    k_st = (kf * jnp.exp2(bl - b)).astype(BF16)
    q32 = (qf * jnp.exp2(b - p32)).astype(BF16)
    k32 = (kf * jnp.exp2(p32 - b)).astype(BF16)
    q16 = (qf * jnp.exp2(b - p16)).astype(BF16)
    k16 = (kf * jnp.exp2(p16 - b)).astype(BF16)
    q8 = (qf * jnp.exp2(b - p8)).astype(BF16)
    k8 = (kf * jnp.exp2(p8 - b)).astype(BF16)
    col = lax.broadcasted_iota(jnp.int32, (SUB, CHUNK), 1)
    row_scr[0] = kf
    row_scr[1] = b

    outs = []
    for c in range(nch):
        rs_ = slice(c * CHUNK, (c + 1) * CHUNK)
        diag = []
        for g in range(CHUNK // SUB):
            r0 = c * CHUNK + g * SUB
            qb = qf[r0:r0 + SUB]
            bb = b[r0:r0 + SUB]
            a_g = jnp.zeros((SUB, CHUNK), F32)
            for j in range(SUB):
                k_j = row_scr[0, r0 + j:r0 + j + 1, :]
                b_j = row_scr[1, r0 + j:r0 + j + 1, :]
                p = qb * k_j * jnp.exp2(bb - b_j)
                a_g = jnp.where(col == g * SUB + j, jnp.sum(p, axis=1, keepdims=True), a_g)
            diag.append(a_g)
        a = jnp.where(md, jnp.concatenate(diag, axis=0), 0.0)
        a = jnp.where(m8, _nt_dot(q8[rs_], k8[rs_]), a)
        a = jnp.where(m16, _nt_dot(q16[rs_], k16[rs_]), a)
        a = jnp.where(m32, _nt_dot(q32[rs_], k32[rs_]), a)
        v_c = vb[rs_]
        o = jnp.dot(a.astype(BF16), v_c, preferred_element_type=F32) + _nt_dot(q_in[rs_], st.astype(BF16))
        outs.append(o)
        upd = lax.dot_general(v_c, k_st[rs_], (((0,), (0,)), ((), ())), preferred_element_type=F32)
        st = st * jnp.exp2(bl[c * CHUNK:c * CHUNK + 1]) + upd
    return jnp.concatenate(outs, axis=0), st


def _mix_kernel(q_ref, k_ref, v_ref, la_ref, rs_ref, cs_ref, sgg_ref, sgc_ref, x_ref,
                gn_ref, s0_ref, wgo_ref, wco_ref, wout_ref, h1_ref, sfin_ref, *scratch, nch):
    st_scrs, row_scrs = scratch[:GLA_HEADS], scratch[GLA_HEADS:]
    t = pl.program_id(1)

    @pl.when(t == 0)
    def _():
        for h in range(GLA_HEADS):
            st_scrs[h][...] = s0_ref[h]

    cum_mat, masks = _gla_masks()
    merged = sgc_ref[0].astype(F32) * jnp.dot(cs_ref[0], wco_ref[...], preferred_element_type=F32)
    parts = []
    for h in range(GLA_HEADS):
        kc = slice(h * GLA_DKH, (h + 1) * GLA_DKH)
        vc = slice(h * GLA_DVH, (h + 1) * GLA_DVH)
        o, st = _gla_head(q_ref[0, :, kc].astype(F32), k_ref[0, :, kc].astype(F32), v_ref[0, :, vc],
                          la_ref[0, :, kc], st_scrs[h][...], cum_mat, masks, nch, row_scrs[h])
        st_scrs[h][...] = st
        ms = jnp.mean(o * o, axis=-1, keepdims=True)
        on = o * lax.rsqrt(ms + RMS_EPS) * gn_ref[h]
        og = (on * rs_ref[0, :, vc].astype(F32)).astype(BF16)
        parts.append(jnp.dot(og, wgo_ref[vc, :], preferred_element_type=F32))

    br_gla = (parts[0] + parts[1]) + (parts[2] + parts[3])
    merged = (merged + sgg_ref[0].astype(F32) * br_gla).astype(BF16)
    h1_ref[0] = x_ref[0] + jnp.dot(merged, wout_ref[...], preferred_element_type=F32)

    @pl.when(t == pl.num_programs(1) - 1)
    def _():
        for h in range(GLA_HEADS):
            sfin_ref[0, h] = st_scrs[h][...]


def _mix(q, k, v, la, rs, cs, sgg, sgc, x, gn, s0, wgo, wco, wout, tb):
    B, S, D = x.shape
    grid = (B, S // tb)
    row = lambda n: pl.BlockSpec((1, tb, n), lambda b, t: (b, t, 0))
    st_shape = (GLA_HEADS, GLA_DVH, GLA_DKH)
    return pl.pallas_call(
        functools.partial(_mix_kernel, nch=tb // CHUNK),
        grid=grid,
        in_specs=[row(GLA_DK), row(GLA_DK), row(GLA_DV), row(GLA_DK), row(GLA_DV), row(D), row(D), row(D), row(D),
                  _const_spec((GLA_HEADS, 1, GLA_DVH)), _const_spec(st_shape),
                  _const_spec((GLA_DV, D)), _const_spec((D, D)), _const_spec((D, D))],
        out_specs=[row(D), pl.BlockSpec((1,) + st_shape, lambda b, t: (b, 0, 0, 0))],
        out_shape=[jax.ShapeDtypeStruct((B, S, D), F32), jax.ShapeDtypeStruct((B,) + st_shape, F32)],
        scratch_shapes=([pltpu.VMEM(st_shape[1:], F32)] * GLA_HEADS
                        + [pltpu.VMEM((2, tb, GLA_DKH), F32)] * GLA_HEADS),
        compiler_params=_params(("parallel", "arbitrary")),
        name="mix",
    )(q, k, v, la, rs, cs, sgg, sgc, x, gn, s0, wgo, wco, wout)


def _ffn_kernel(h1_ref, g_ref, wup_ref, dww_ref, dwb_ref, wdn_ref, gfin_ref, atail0_ref,
                out_ref, atail_ref, win_scr, *, tm):
    i = pl.program_id(1)

    @pl.when(i == 0)
    def _():
        win_scr[0:FFN_HIST, :] = atail0_ref[...]

    h1 = h1_ref[0]
    ms = jnp.mean(h1 * h1, axis=-1, keepdims=True)
    u = (h1 * lax.rsqrt(ms + RMS_EPS) * g_ref[...]).astype(BF16)
    a = jnp.dot(u, wup_ref[:, 0:D_FF], preferred_element_type=F32)
    bv = jnp.dot(u, wup_ref[:, D_FF:2 * D_FF], preferred_element_type=F32)
    win_scr[FFN_HIST:FFN_HIST + tm, :] = a
    off0 = FFN_HIST - (FFN_K - 1)
    ac = dwb_ref[...] + dww_ref[0:1, :] * win_scr[off0:off0 + tm, :]
    for j in range(1, FFN_K):
        ac = ac + dww_ref[j:j + 1, :] * win_scr[off0 + j:off0 + j + tm, :]
    hdn = (ac * _sigmoid(ac) * bv).astype(BF16)
    h2 = h1 + jnp.dot(hdn, wdn_ref[...], preferred_element_type=F32)
    ms2 = jnp.mean(h2 * h2, axis=-1, keepdims=True)
    out_ref[0] = h2 * lax.rsqrt(ms2 + RMS_EPS) * gfin_ref[...]
    tail = win_scr[tm:tm + FFN_HIST, :]
    win_scr[0:FFN_HIST, :] = tail
    atail_ref[0] = tail


def _ffn(h1, g, wup, dww, dwb, wdn, gfin, atail0, tm):
    B, S, D = h1.shape
    grid = (B, S // tm)
    row = pl.BlockSpec((1, tm, D), lambda b, i: (b, i, 0))
    return pl.pallas_call(
        functools.partial(_ffn_kernel, tm=tm),
        grid=grid,
        in_specs=[row, _const_spec((1, D)), _const_spec((D, 2 * D_FF)), _const_spec((FFN_K, D_FF)),
                  _const_spec((1, D_FF)), _const_spec((D_FF, D)), _const_spec((1, D)),
                  _const_spec((FFN_HIST, D_FF))],
        out_specs=[row, pl.BlockSpec((1, FFN_HIST, D_FF), lambda b, i: (b, 0, 0))],
        out_shape=[jax.ShapeDtypeStruct((B, S, D), F32), jax.ShapeDtypeStruct((B, FFN_HIST, D_FF), F32)],
        scratch_shapes=[pltpu.VMEM((FFN_HIST + tm, D_FF), F32)],
        compiler_params=_params(("parallel", "arbitrary")),
        name="ffn",
    )(h1, g, wup, dww, dwb, wdn, gfin, atail0)


def _block(h, carries, wts, tiles, la_mask_rows=0):
    s0, ctail0, atail0 = carries
    tm_proj, tb_mix, tm_ffn = tiles
    q, k, v, rs, la, cs, sgg, sgc, ctail = _proj(
        h, wts["g_mix"], wts["w_a"], wts["w_b"], wts["w_alr"], wts["w_au"], wts["b_al"],
        wts["dw_w"], wts["dw_b"], wts["ln_g"], wts["ln_b"], ctail0, tm_proj)
    if la_mask_rows:
        la = la.at[:, :la_mask_rows].set(0.0)
    h1, sfin = _mix(q, k, v, la, rs, cs, sgg, sgc, h, wts["g_gla"], s0,
                    wts["w_gla_o"], wts["w_conf_o"], wts["w_out"], tb_mix)
    out, atail = _ffn(h1, wts["g_ffn"], wts["w_up"], wts["ffn_dw_w"], wts["ffn_dw_b"], wts["w_down"],
                      wts["g_fin"], atail0, tm_ffn)
    return out, (sfin[0], ctail[0], atail[0])


def kernel(x, meta_tokens, norm_mix_g, w_in, w_alpha_up, b_alpha, gla_norm_g, w_gla_o, conf_dw_w, conf_dw_b,
           conf_ln_g, conf_ln_b, w_conf_o, w_out, norm_ffn_g, w_up, ffn_dw_w, ffn_dw_b, w_down, final_norm_g):
    assert norm_mix_g.shape[0] == 1, "single-layer block"
    B, S, D = x.shape
    lr0 = 2 * GLA_DK + 2 * GLA_DV
    w_in0 = w_in[0].astype(BF16)
    wts = dict(
        g_mix=norm_mix_g[0][None],
        w_a=w_in0[:, :lr0], w_b=w_in0[:, lr0 + GLA_RANK:],
        w_alr=jnp.pad(w_in0[:, lr0:lr0 + GLA_RANK], ((0, 0), (0, RANK_PAD - GLA_RANK))),
        w_au=jnp.pad(w_alpha_up[0], ((0, RANK_PAD - GLA_RANK), (0, 0))).astype(BF16),
        b_al=b_alpha[0][None],
        g_gla=gla_norm_g[0].reshape(GLA_HEADS, 1, GLA_DVH),
        dw_w=conf_dw_w[0], dw_b=conf_dw_b[0][None], ln_g=conf_ln_g[0][None], ln_b=conf_ln_b[0][None],
        w_gla_o=w_gla_o[0].astype(BF16), w_conf_o=w_conf_o[0].astype(BF16), w_out=w_out[0].astype(BF16),
        g_ffn=norm_ffn_g[0][None], w_up=w_up[0].astype(BF16),
        ffn_dw_w=ffn_dw_w[0], ffn_dw_b=ffn_dw_b[0][None], w_down=w_down[0].astype(BF16),
        g_fin=final_norm_g[None],
    )
    zero_carries = (jnp.zeros((GLA_HEADS, GLA_DVH, GLA_DKH), F32),
                    jnp.zeros((CONV_HIST, D), F32),
                    jnp.zeros((FFN_HIST, D_FF), F32))
    h_meta = jnp.concatenate([jnp.zeros((META_PAD, D), x.dtype), meta_tokens.astype(x.dtype)], axis=0)[None]
    _, carries = _block(h_meta, zero_carries, wts, (CHUNK, CHUNK, CHUNK), la_mask_rows=META_PAD)
    out, _ = _block(x, carries, wts, (SEQ_TILE,) * 3)
    return out
```

```python
import functools

import jax
import jax.numpy as jnp
from jax import lax
from jax.experimental import pallas as pl
from jax.experimental.pallas import tpu as pltpu

F32 = jnp.float32
BF16 = jnp.bfloat16

D_MODEL = 1024
N_META = 16
GLA_HEADS = 4
GLA_DK = 512
GLA_DV = 1024
GLA_DKH = GLA_DK // GLA_HEADS
GLA_DVH = GLA_DV // GLA_HEADS
GLA_RANK = 16
GATE_TAU = 16.0
LOG2_E = 1.4426950408889634
CHUNK = 128
SUB = 8
GLA_LEVELS = tuple(CHUNK >> s for s in range(1, CHUNK.bit_length()) if CHUNK >> s >= SUB)
META_PAD = CHUNK - N_META
CONF_K = 31
CONV_HIST = 32
CONV_RB = 64
D_FF = 2816
FFN_K = 3
FFN_HIST = 8
RMS_EPS = 1e-6
LN_EPS = 1e-5
LANES = 128
SUBLANES = 8
RANK_PAD = LANES
VMEM_LIMIT = 60 * 1024 * 1024
SEQ_TILE = 512

_COLS = dict(q=(0, 0, 512), k=(0, 512, 1024), v=(0, 1024, 2048), r=(0, 2048, 3072),
             c1=(1, 0, 1024), c2=(1, 1024, 2048), gg=(1, 2048, 3072), gc=(1, 3072, 4096))
N_GROUP = (3072, 4096)


def _sigmoid(x):
    return 1.0 / (1.0 + jnp.exp(-x))


def _log_sigmoid(x):
    return jnp.minimum(x, 0.0) - jnp.log(1.0 + jnp.exp(-jnp.abs(x)))


def _const_spec(shape):
    nd = len(shape)
    return pl.BlockSpec(shape, lambda *_: (0,) * nd, pipeline_mode=pl.Buffered(1))


def _params(sem):
    return pltpu.CompilerParams(dimension_semantics=sem, vmem_limit_bytes=VMEM_LIMIT)


def _zero_after(v):
    bits = lax.bitcast_convert_type(v, jnp.uint32)
    half_word = jnp.uint32(16)
    bits = lax.shift_right_logical(lax.shift_right_logical(bits, half_word), half_word)
    return lax.bitcast_convert_type(bits, F32)


def _conv31_block(cb, win_scr, sh_scr, dww_ref, dwb_ref, conv_scr, tm, after):
    nw = CONV_HIST + tm
    off0 = CONV_HIST - (CONF_K - 1)
    cols = slice(cb * LANES, (cb + 1) * LANES)
    sh = sh_scr.at[cb % 2]
    wblk = win_scr[:, cols]
    for r in range(1, SUBLANES):
        sh[r - 1] = pltpu.roll(wblk, nw - r, 0)
    bias = jnp.broadcast_to(dwb_ref[:, cols], (SUBLANES, LANES)) + _zero_after(after)
    for rb in range(tm // CONV_RB):
        acc = jnp.concatenate([bias] * (CONV_RB // SUBLANES), axis=0)
        for j in range(CONF_K):
            r = (off0 + j) % SUBLANES
            start = rb * CONV_RB + (off0 + j) - r
            if r == 0:
                src = win_scr[start:start + CONV_RB, cols]
            else:
                src = sh[r - 1, start:start + CONV_RB, :]
            acc = acc + dww_ref[j:j + 1, cols] * src
        conv_scr[rb * CONV_RB:(rb + 1) * CONV_RB, cols] = acc
    return acc[CONV_RB - SUBLANES:CONV_RB]


def _proj_kernel(x_ref, g_ref, wa_ref, wb_ref, walr_ref, wau_ref, bal_ref, dww_ref, dwb_ref, lng_ref, lnb_ref,
                 ctail0_ref,
                 q_ref, k_ref, v_ref, rs_ref, la_ref, cs_ref, sgg_ref, sgc_ref, ctail_ref,
                 win_scr, sh_scr, conv_scr, *, tm):
    i = pl.program_id(1)

    @pl.when(i == 0)
    def _():
        win_scr[0:CONV_HIST, :] = ctail0_ref[...]

    x = x_ref[0]
    ms = jnp.mean(x * x, axis=-1, keepdims=True)
    u = (x * lax.rsqrt(ms + RMS_EPS) * g_ref[...]).astype(BF16)

    def mm(name, part=0, nparts=1):
        group, lo, hi = _COLS[name]
        w_ref = (wa_ref, wb_ref)[group]
        w = (hi - lo) // nparts
        return jnp.dot(u, w_ref[:, lo + part * w:lo + (part + 1) * w], preferred_element_type=F32)

    def half(n, part):
        return slice(part * (n // 2), (part + 1) * (n // 2))

    def last_tile(res, after):
        tile = res[tm - SUBLANES:tm, 0:LANES]
        return tile if after is None else tile + _zero_after(after)

    n_glu = 4
    glu_w = D_MODEL // n_glu
    n_blocks = D_MODEL // LANES
    n_norm = 4
    norm_rows = tm // n_norm

    def p_glu(part, after):
        c1 = mm("c1", part, n_glu)
        c2 = mm("c2", part, n_glu)
        win_scr[CONV_HIST:CONV_HIST + tm, part * glu_w:(part + 1) * glu_w] = c1 * _sigmoid(c2)
        return last_tile(c2, after)

    def p_q(after):
        res = mm("q")
        q_ref[0] = (res * (GLA_DKH ** -0.5)).astype(BF16)
        return last_tile(res, after)

    def p_k(after):
        res = mm("k")
        k_ref[0] = res.astype(BF16)
        return last_tile(res, after)

    def p_v(part, after):
        res = mm("v", part, 2)
        v_ref[0, :, half(GLA_DV, part)] = res.astype(BF16)
        return last_tile(res, after)

    def p_r(part, after):
        r = mm("r", part, 2)
        rs_ref[0, :, half(GLA_DV, part)] = (r * _sigmoid(r)).astype(BF16)
        return last_tile(r, after)

    def p_gate(name, ref, part, after):
        res = mm(name, part, 2)
        ref[0, :, half(D_MODEL, part)] = _sigmoid(res).astype(BF16)
        return last_tile(res, after)

    def p_la(after):
        alr = jnp.dot(u, walr_ref[...], preferred_element_type=F32)
        xa = jnp.dot(alr.astype(BF16), wau_ref[...], preferred_element_type=F32) + bal_ref[...]
        la_ref[0] = _log_sigmoid(xa) * (1.0 / GATE_TAU)
        return last_tile(xa, after)

    def u_conv(cb, after):
        return _conv31_block(cb, win_scr, sh_scr, dww_ref, dwb_ref, conv_scr, tm, after)

    def u_norm(part, after):
        rows = slice(part * norm_rows, (part + 1) * norm_rows)
        zero_row = jnp.concatenate([_zero_after(after)[0:1]] * n_blocks, axis=1)
        cv = conv_scr[rows, :] + zero_row
        mu = jnp.mean(cv, axis=-1, keepdims=True)
        xc = cv - mu
        var = jnp.mean(xc * xc, axis=-1, keepdims=True)
        y = xc * lax.rsqrt(var + LN_EPS) * lng_ref[...] + lnb_ref[...]
        cs_ref[0, rows, :] = (y * _sigmoid(y)).astype(BF16)
        return y[norm_rows - SUBLANES:norm_rows, 0:LANES]

    P = functools.partial
    pieces = [P(p_glu, 0), P(p_glu, 1), p_la, P(p_glu, 2), P(p_glu, 3), p_q, p_k, P(p_v, 0), P(p_v, 1),
              P(p_r, 0), P(p_r, 1), P(p_gate, "gg", sgg_ref, 0), P(p_gate, "gg", sgg_ref, 1),
              P(p_gate, "gc", sgc_ref, 0), P(p_gate, "gc", sgc_ref, 1)]
    units = [P(u_conv, cb) for cb in range(n_blocks)] + [P(u_norm, part) for part in range(n_norm)]
    groups = [[0], [1], [2, 3], [4], [5, 6], [7], [8, 9], [10], [11], [12], [13], [14]]
    folds = {groups[j + 1][-1]: j for j in range(len(units) - 1)}
    unit_tiles = []
    for j, unit in enumerate(units):
        tiles = [pieces[m](unit_tiles[folds[m]] if m in folds else None) for m in groups[j]]
        unit_tiles.append(unit(functools.reduce(lambda a, b: a + b, tiles)))
        if j == n_blocks - 1:
            tail = win_scr[tm:tm + CONV_HIST, :]
            win_scr[0:CONV_HIST, :] = tail
            ctail_ref[0] = tail


def _proj(x, g, w_a, w_b, w_alr, w_au, b_al, dww, dwb, lng, lnb, ctail0, tm):
    B, S, D = x.shape
    grid = (B, S // tm)
    row = lambda n: pl.BlockSpec((1, tm, n), lambda b, i: (b, i, 0))
    bf = lambda n: jax.ShapeDtypeStruct((B, S, n), BF16)
    return pl.pallas_call(
        functools.partial(_proj_kernel, tm=tm),
        grid=grid,
        in_specs=[row(D), _const_spec((1, D)), _const_spec((D, N_GROUP[0])), _const_spec((D, N_GROUP[1])),
                  _const_spec((D, RANK_PAD)),
                  _const_spec((RANK_PAD, GLA_DK)), _const_spec((1, GLA_DK)),
                  _const_spec((CONF_K, D)), _const_spec((1, D)), _const_spec((1, D)), _const_spec((1, D)),
                  _const_spec((CONV_HIST, D))],
        out_specs=[row(GLA_DK), row(GLA_DK), row(GLA_DV), row(GLA_DV), row(GLA_DK), row(D), row(D), row(D),
                   pl.BlockSpec((1, CONV_HIST, D), lambda b, i: (b, 0, 0))],
        out_shape=[bf(GLA_DK), bf(GLA_DK), bf(GLA_DV), bf(GLA_DV),
                   jax.ShapeDtypeStruct((B, S, GLA_DK), F32), bf(D), bf(D), bf(D),
                   jax.ShapeDtypeStruct((B, CONV_HIST, D), F32)],
        scratch_shapes=[pltpu.VMEM((CONV_HIST + tm, D), F32),
                        pltpu.VMEM((2, SUBLANES - 1, CONV_HIST + tm, LANES), F32),
                        pltpu.VMEM((tm, D), F32)],
        compiler_params=_params(("parallel", "arbitrary")),
        name="proj",
    )(x, g, w_a, w_b, w_alr, w_au, b_al, dww, dwb, lng, lnb, ctail0)


def _gla_masks():
    i_ = lax.broadcasted_iota(jnp.int32, (CHUNK, CHUNK), 0)
    j_ = lax.broadcasted_iota(jnp.int32, (CHUNK, CHUNK), 1)
    sh = lambda x, n: jnp.right_shift(x, n)
    log2 = lambda n: n.bit_length() - 1
    one = jnp.ones((CHUNK, CHUNK), F32)
    zero = jnp.zeros((CHUNK, CHUNK), F32)
    sel = lambda m: jnp.where(m, one, zero)
    up = lambda x, n: jnp.bitwise_and(sh(x, n), 1) == 1
    pivots, level_masks = [], []
    for lvl in GLA_LEVELS:
        blk = log2(2 * lvl)
        pivots.append(sel(j_ < jnp.left_shift(sh(i_, blk), blk) + lvl))
        level_masks.append(sel(up(i_, log2(lvl))) * sel(~up(j_, log2(lvl))) * sel(sh(i_, blk) == sh(j_, blk)) > 0.5)
    cum_mat = jnp.concatenate([sel(j_ <= i_)] + pivots + [one], axis=0).astype(BF16)
    md = sel(sh(i_, log2(SUB)) == sh(j_, log2(SUB))) * sel(j_ <= i_) > 0.5
    return cum_mat, (level_masks, md)


def _nt_dot(a, b):
    return lax.dot_general(a, b, (((1,), (1,)), ((), ())), preferred_element_type=F32)


def _gla_head(qf, kf, vb, la, st, cum_mat, masks, nch, row_scr):
    level_masks, md = masks
    n_rows = len(GLA_LEVELS) + 2
    la = la * LOG2_E
    la_hi = la.astype(BF16)
    la_lo = (la - la_hi.astype(F32)).astype(BF16)
    la2 = jnp.concatenate([la_hi, la_lo], axis=1)
    parts = [[] for _ in range(n_rows)]
    for c in range(nch):
        cs = jnp.dot(cum_mat, la2[c * CHUNK:(c + 1) * CHUNK], preferred_element_type=F32)
        cs = cs[:, :GLA_DKH] + cs[:, GLA_DKH:]
        for n in range(n_rows):
            parts[n].append(cs[n * CHUNK:(n + 1) * CHUNK])
    b, *pivots, bl = (jnp.concatenate(p, axis=0) for p in parts)

    q_in = (qf * jnp.exp2(b)).astype(BF16)
    k_st = (kf * jnp.exp2(bl - b)).astype(BF16)
    q_lvl = [(qf * jnp.exp2(b - p)).astype(BF16) for p in pivots]
    k_lvl = [(kf * jnp.exp2(p - b)).astype(BF16) for p in pivots]
    col = lax.broadcasted_iota(jnp.int32, (SUB, CHUNK), 1)
    row_scr[0] = kf
    row_scr[1] = b

    outs = []
    for c in range(nch):
        rs_ = slice(c * CHUNK, (c + 1) * CHUNK)
        diag = []
        for g in range(CHUNK // SUB):
            r0 = c * CHUNK + g * SUB
            qb = qf[r0:r0 + SUB]
            bb = b[r0:r0 + SUB]
            a_g = jnp.zeros((SUB, CHUNK), F32)
            for j in range(SUB):
                k_j = row_scr[0, r0 + j:r0 + j + 1, :]
                b_j = row_scr[1, r0 + j:r0 + j + 1, :]
                p = qb * k_j * jnp.exp2(bb - b_j)
                a_g = jnp.where(col == g * SUB + j, jnp.sum(p, axis=1, keepdims=True), a_g)
            diag.append(a_g)
        a = jnp.where(md, jnp.concatenate(diag, axis=0), 0.0)
        for mask, q_l, k_l in zip(level_masks, q_lvl, k_lvl):
            a = jnp.where(mask, _nt_dot(q_l[rs_], k_l[rs_]), a)
        v_c = vb[rs_]
        o = jnp.dot(a.astype(BF16), v_c, preferred_element_type=F32) + _nt_dot(q_in[rs_], st.astype(BF16))
        outs.append(o)
        upd = lax.dot_general(v_c, k_st[rs_], (((0,), (0,)), ((), ())), preferred_element_type=F32)
        st = st * jnp.exp2(bl[c * CHUNK:c * CHUNK + 1]) + upd
    return jnp.concatenate(outs, axis=0), st


def _mix_kernel(q_ref, k_ref, v_ref, la_ref, rs_ref, cs_ref, sgg_ref, sgc_ref, x_ref,
                gn_ref, s0_ref, wgo_ref, wco_ref, wout_ref, h1_ref, sfin_ref, *scratch, nch):
    st_scrs, row_scrs = scratch[:GLA_HEADS], scratch[GLA_HEADS:]
    t = pl.program_id(1)

    @pl.when(t == 0)
    def _():
        for h in range(GLA_HEADS):
            st_scrs[h][...] = s0_ref[h]

    cum_mat, masks = _gla_masks()
    merged = sgc_ref[0].astype(F32) * jnp.dot(cs_ref[0], wco_ref[...], preferred_element_type=F32)
    parts = []
    for h in range(GLA_HEADS):
        kc = slice(h * GLA_DKH, (h + 1) * GLA_DKH)
        vc = slice(h * GLA_DVH, (h + 1) * GLA_DVH)
        o, st = _gla_head(q_ref[0, :, kc].astype(F32), k_ref[0, :, kc].astype(F32), v_ref[0, :, vc],
                          la_ref[0, :, kc], st_scrs[h][...], cum_mat, masks, nch, row_scrs[h])
        st_scrs[h][...] = st
        ms = jnp.mean(o * o, axis=-1, keepdims=True)
        on = o * lax.rsqrt(ms + RMS_EPS) * gn_ref[h]
        og = (on * rs_ref[0, :, vc].astype(F32)).astype(BF16)
        parts.append(jnp.dot(og, wgo_ref[vc, :], preferred_element_type=F32))

    br_gla = (parts[0] + parts[1]) + (parts[2] + parts[3])
    merged = (merged + sgg_ref[0].astype(F32) * br_gla).astype(BF16)
    h1_ref[0] = x_ref[0] + jnp.dot(merged, wout_ref[...], preferred_element_type=F32)

    @pl.when(t == pl.num_programs(1) - 1)
    def _():
        for h in range(GLA_HEADS):
            sfin_ref[0, h] = st_scrs[h][...]


def _mix(q, k, v, la, rs, cs, sgg, sgc, x, gn, s0, wgo, wco, wout, tb):
    B, S, D = x.shape
    grid = (B, S // tb)
    row = lambda n: pl.BlockSpec((1, tb, n), lambda b, t: (b, t, 0))
    st_shape = (GLA_HEADS, GLA_DVH, GLA_DKH)
    return pl.pallas_call(
        functools.partial(_mix_kernel, nch=tb // CHUNK),
        grid=grid,
        in_specs=[row(GLA_DK), row(GLA_DK), row(GLA_DV), row(GLA_DK), row(GLA_DV), row(D), row(D), row(D), row(D),
                  _const_spec((GLA_HEADS, 1, GLA_DVH)), _const_spec(st_shape),
                  _const_spec((GLA_DV, D)), _const_spec((D, D)), _const_spec((D, D))],
        out_specs=[row(D), pl.BlockSpec((1,) + st_shape, lambda b, t: (b, 0, 0, 0))],
        out_shape=[jax.ShapeDtypeStruct((B, S, D), F32), jax.ShapeDtypeStruct((B,) + st_shape, F32)],
        scratch_shapes=([pltpu.VMEM(st_shape[1:], F32)] * GLA_HEADS
                        + [pltpu.VMEM((2, tb, GLA_DKH), F32)] * GLA_HEADS),
        compiler_params=_params(("parallel", "arbitrary")),
        name="mix",
    )(q, k, v, la, rs, cs, sgg, sgc, x, gn, s0, wgo, wco, wout)


def _ffn_kernel(h1_ref, g_ref, wup_ref, dww_ref, dwb_ref, wdn_ref, gfin_ref, atail0_ref,
                out_ref, atail_ref, win_scr, *, tm):
    i = pl.program_id(1)

    @pl.when(i == 0)
    def _():
        win_scr[0:FFN_HIST, :] = atail0_ref[...]

    h1 = h1_ref[0]
    ms = jnp.mean(h1 * h1, axis=-1, keepdims=True)
    u = (h1 * lax.rsqrt(ms + RMS_EPS) * g_ref[...]).astype(BF16)
    a = jnp.dot(u, wup_ref[:, 0:D_FF], preferred_element_type=F32)
    bv = jnp.dot(u, wup_ref[:, D_FF:2 * D_FF], preferred_element_type=F32)
    win_scr[FFN_HIST:FFN_HIST + tm, :] = a
    off0 = FFN_HIST - (FFN_K - 1)
    ac = dwb_ref[...] + dww_ref[0:1, :] * win_scr[off0:off0 + tm, :]
    for j in range(1, FFN_K):
        ac = ac + dww_ref[j:j + 1, :] * win_scr[off0 + j:off0 + j + tm, :]
    hdn = (ac * _sigmoid(ac) * bv).astype(BF16)
    h2 = h1 + jnp.dot(hdn, wdn_ref[...], preferred_element_type=F32)
    ms2 = jnp.mean(h2 * h2, axis=-1, keepdims=True)
    out_ref[0] = h2 * lax.rsqrt(ms2 + RMS_EPS) * gfin_ref[...]
    tail = win_scr[tm:tm + FFN_HIST, :]
    win_scr[0:FFN_HIST, :] = tail
    atail_ref[0] = tail


def _ffn(h1, g, wup, dww, dwb, wdn, gfin, atail0, tm):
    B, S, D = h1.shape
    grid = (B, S // tm)
    row = pl.BlockSpec((1, tm, D), lambda b, i: (b, i, 0))
    return pl.pallas_call(
        functools.partial(_ffn_kernel, tm=tm),
        grid=grid,
        in_specs=[row, _const_spec((1, D)), _const_spec((D, 2 * D_FF)), _const_spec((FFN_K, D_FF)),
                  _const_spec((1, D_FF)), _const_spec((D_FF, D)), _const_spec((1, D)),
                  _const_spec((FFN_HIST, D_FF))],
        out_specs=[row, pl.BlockSpec((1, FFN_HIST, D_FF), lambda b, i: (b, 0, 0))],
        out_shape=[jax.ShapeDtypeStruct((B, S, D), F32), jax.ShapeDtypeStruct((B, FFN_HIST, D_FF), F32)],
        scratch_shapes=[pltpu.VMEM((FFN_HIST + tm, D_FF), F32)],
        compiler_params=_params(("parallel", "arbitrary")),
        name="ffn",
    )(h1, g, wup, dww, dwb, wdn, gfin, atail0)


def _block(h, carries, wts, tiles, la_mask_rows=0):
    s0, ctail0, atail0 = carries
    tm_proj, tb_mix, tm_ffn = tiles
    q, k, v, rs, la, cs, sgg, sgc, ctail = _proj(
        h, wts["g_mix"], wts["w_a"], wts["w_b"], wts["w_alr"], wts["w_au"], wts["b_al"],
        wts["dw_w"], wts["dw_b"], wts["ln_g"], wts["ln_b"], ctail0, tm_proj)
    if la_mask_rows:
        la = la.at[:, :la_mask_rows].set(0.0)
    h1, sfin = _mix(q, k, v, la, rs, cs, sgg, sgc, h, wts["g_gla"], s0,
                    wts["w_gla_o"], wts["w_conf_o"], wts["w_out"], tb_mix)
    out, atail = _ffn(h1, wts["g_ffn"], wts["w_up"], wts["ffn_dw_w"], wts["ffn_dw_b"], wts["w_down"],
                      wts["g_fin"], atail0, tm_ffn)
    return out, (sfin[0], ctail[0], atail[0])


def kernel(x, meta_tokens, norm_mix_g, w_in, w_alpha_up, b_alpha, gla_norm_g, w_gla_o, conf_dw_w, conf_dw_b,
           conf_ln_g, conf_ln_b, w_conf_o, w_out, norm_ffn_g, w_up, ffn_dw_w, ffn_dw_b, w_down, final_norm_g):
    assert norm_mix_g.shape[0] == 1, "single-layer block"
    B, S, D = x.shape
    lr0 = 2 * GLA_DK + 2 * GLA_DV
    w_in0 = w_in[0].astype(BF16)
    wts = dict(
        g_mix=norm_mix_g[0][None],
        w_a=w_in0[:, :lr0], w_b=w_in0[:, lr0 + GLA_RANK:],
        w_alr=jnp.pad(w_in0[:, lr0:lr0 + GLA_RANK], ((0, 0), (0, RANK_PAD - GLA_RANK))),
        w_au=jnp.pad(w_alpha_up[0], ((0, RANK_PAD - GLA_RANK), (0, 0))).astype(BF16),
        b_al=b_alpha[0][None],
        g_gla=gla_norm_g[0].reshape(GLA_HEADS, 1, GLA_DVH),
        dw_w=conf_dw_w[0], dw_b=conf_dw_b[0][None], ln_g=conf_ln_g[0][None], ln_b=conf_ln_b[0][None],
        w_gla_o=w_gla_o[0].astype(BF16), w_conf_o=w_conf_o[0].astype(BF16), w_out=w_out[0].astype(BF16),
        g_ffn=norm_ffn_g[0][None], w_up=w_up[0].astype(BF16),
        ffn_dw_w=ffn_dw_w[0], ffn_dw_b=ffn_dw_b[0][None], w_down=w_down[0].astype(BF16),
        g_fin=final_norm_g[None],
    )
    zero_carries = (jnp.zeros((GLA_HEADS, GLA_DVH, GLA_DKH), F32),
                    jnp.zeros((CONV_HIST, D), F32),
                    jnp.zeros((FFN_HIST, D_FF), F32))
    h_meta = jnp.concatenate([jnp.zeros((META_PAD, D), x.dtype), meta_tokens.astype(x.dtype)], axis=0)[None]
    _, carries = _block(h_meta, zero_carries, wts, (CHUNK, CHUNK, CHUNK), la_mask_rows=META_PAD)
    out, _ = _block(x, carries, wts, (SEQ_TILE,) * 3)
    return out
```

```python
import functools

import jax
import jax.numpy as jnp
from jax import lax
from jax.experimental import pallas as pl
from jax.experimental.pallas import tpu as pltpu

F32 = jnp.float32
BF16 = jnp.bfloat16

D_MODEL = 1024
N_META = 16
GLA_HEADS = 4
GLA_DK = 512
GLA_DV = 1024
GLA_DKH = GLA_DK // GLA_HEADS
GLA_DVH = GLA_DV // GLA_HEADS
GLA_RANK = 16
GATE_TAU = 16.0
LOG2_E = 1.4426950408889634
CHUNK = 128
SUB = 8
GLA_LEVELS = tuple(CHUNK >> s for s in range(1, CHUNK.bit_length()) if CHUNK >> s >= SUB)
META_PAD = CHUNK - N_META
CONF_K = 31
CONV_HIST = 32
CONV_RB = 64
D_FF = 2816
FFN_K = 3
FFN_HIST = 8
RMS_EPS = 1e-6
LN_EPS = 1e-5
LANES = 128
SUBLANES = 8
RANK_PAD = LANES
VMEM_LIMIT = 60 * 1024 * 1024
SEQ_TILE = 512

_COLS = dict(q=(0, 0, 512), k=(0, 512, 1024), v=(0, 1024, 2048), r=(0, 2048, 3072),
             c1=(1, 0, 1024), c2=(1, 1024, 2048), gg=(1, 2048, 3072), gc=(1, 3072, 4096))
N_GROUP = (3072, 4096)


def _sigmoid(x):
    return 1.0 / (1.0 + jnp.exp(-x))


def _log_sigmoid(x):
    return jnp.minimum(x, 0.0) - jnp.log(1.0 + jnp.exp(-jnp.abs(x)))


def _const_spec(shape):
    nd = len(shape)
    return pl.BlockSpec(shape, lambda *_: (0,) * nd, pipeline_mode=pl.Buffered(1))


def _params(sem):
    return pltpu.CompilerParams(dimension_semantics=sem, vmem_limit_bytes=VMEM_LIMIT)


def _zero_after(v):
    bits = lax.bitcast_convert_type(v, jnp.uint32)
    half_word = jnp.uint32(16)
    bits = lax.shift_right_logical(lax.shift_right_logical(bits, half_word), half_word)
    return lax.bitcast_convert_type(bits, F32)


def _conv31_block(cb, win_scr, sh_scr, dww_ref, dwb_ref, conv_scr, tm, after):
    nw = CONV_HIST + tm
    off0 = CONV_HIST - (CONF_K - 1)
    cols = slice(cb * LANES, (cb + 1) * LANES)
    sh = sh_scr.at[cb % 2]
    wblk = win_scr[:, cols]
    for r in range(1, SUBLANES):
        sh[r - 1] = pltpu.roll(wblk, nw - r, 0)
    bias = jnp.broadcast_to(dwb_ref[:, cols], (SUBLANES, LANES)) + _zero_after(after)
    for rb in range(tm // CONV_RB):
        acc = jnp.concatenate([bias] * (CONV_RB // SUBLANES), axis=0)
        for j in range(CONF_K):
            r = (off0 + j) % SUBLANES
            start = rb * CONV_RB + (off0 + j) - r
            if r == 0:
                src = win_scr[start:start + CONV_RB, cols]
            else:
                src = sh[r - 1, start:start + CONV_RB, :]
            acc = acc + dww_ref[j:j + 1, cols] * src
        conv_scr[rb * CONV_RB:(rb + 1) * CONV_RB, cols] = acc
    return acc[CONV_RB - SUBLANES:CONV_RB]


def _proj_kernel(x_ref, g_ref, wa_ref, wb_ref, walr_ref, wau_ref, bal_ref, dww_ref, dwb_ref, lng_ref, lnb_ref,
                 ctail0_ref,
                 q_ref, k_ref, v_ref, rs_ref, la_ref, cs_ref, sgg_ref, sgc_ref, ctail_ref,
                 win_scr, sh_scr, conv_scr, *, tm):
    i = pl.program_id(1)

    @pl.when(i == 0)
    def _():
        win_scr[0:CONV_HIST, :] = ctail0_ref[...]

    x = x_ref[0]
    ms = jnp.mean(x * x, axis=-1, keepdims=True)
    u = (x * lax.rsqrt(ms + RMS_EPS) * g_ref[...]).astype(BF16)

    def mm(name, part=0, nparts=1):
        group, lo, hi = _COLS[name]
        w_ref = (wa_ref, wb_ref)[group]
        w = (hi - lo) // nparts
        return jnp.dot(u, w_ref[:, lo + part * w:lo + (part + 1) * w], preferred_element_type=F32)

    def half(n, part):
        return slice(part * (n // 2), (part + 1) * (n // 2))

    def last_tile(res, after):
        tile = res[tm - SUBLANES:tm, 0:LANES]
        return tile if after is None else tile + _zero_after(after)

    n_glu = 4
    glu_w = D_MODEL // n_glu
    n_blocks = D_MODEL // LANES
    n_norm = 4
    norm_rows = tm // n_norm

    def p_glu(part, after):
        c1 = mm("c1", part, n_glu)
        c2 = mm("c2", part, n_glu)
        win_scr[CONV_HIST:CONV_HIST + tm, part * glu_w:(part + 1) * glu_w] = c1 * _sigmoid(c2)
        return last_tile(c2, after)

    def p_q(after):
        res = mm("q")
        q_ref[0] = (res * (GLA_DKH ** -0.5)).astype(BF16)
        return last_tile(res, after)

    def p_k(after):
        res = mm("k")
        k_ref[0] = res.astype(BF16)
        return last_tile(res, after)

    def p_v(part, after):
        res = mm("v", part, 2)
        v_ref[0, :, half(GLA_DV, part)] = res.astype(BF16)
        return last_tile(res, after)

    def p_r(part, after):
        r = mm("r", part, 2)
        rs_ref[0, :, half(GLA_DV, part)] = (r * _sigmoid(r)).astype(BF16)
        return last_tile(r, after)

    def p_gate(name, ref, part, after):
        res = mm(name, part, 2)
        ref[0, :, half(D_MODEL, part)] = _sigmoid(res).astype(BF16)
        return last_tile(res, after)

    def p_la(after):
        alr = jnp.dot(u, walr_ref[...], preferred_element_type=F32)
        xa = jnp.dot(alr.astype(BF16), wau_ref[...], preferred_element_type=F32) + bal_ref[...]
        la_ref[0] = _log_sigmoid(xa) * (1.0 / GATE_TAU)
        return last_tile(xa, after)

    def u_conv(cb, after):
        return _conv31_block(cb, win_scr, sh_scr, dww_ref, dwb_ref, conv_scr, tm, after)

    def u_norm(part, after):
        rows = slice(part * norm_rows, (part + 1) * norm_rows)
        zero_row = jnp.concatenate([_zero_after(after)[0:1]] * n_blocks, axis=1)
        cv = conv_scr[rows, :] + zero_row
        mu = jnp.mean(cv, axis=-1, keepdims=True)
        xc = cv - mu
        var = jnp.mean(xc * xc, axis=-1, keepdims=True)
        y = xc * lax.rsqrt(var + LN_EPS) * lng_ref[...] + lnb_ref[...]
        cs_ref[0, rows, :] = (y * _sigmoid(y)).astype(BF16)
        return y[norm_rows - SUBLANES:norm_rows, 0:LANES]

    P = functools.partial
    pieces = [P(p_glu, 0), P(p_glu, 1), p_la, P(p_glu, 2), P(p_glu, 3), p_q, p_k, P(p_v, 0), P(p_v, 1),
              P(p_r, 0), P(p_r, 1), P(p_gate, "gg", sgg_ref, 0), P(p_gate, "gg", sgg_ref, 1),
              P(p_gate, "gc", sgc_ref, 0), P(p_gate, "gc", sgc_ref, 1)]
    units = [P(u_conv, cb) for cb in range(n_blocks)] + [P(u_norm, part) for part in range(n_norm)]
    groups = [[0], [1], [2, 3], [4], [5, 6], [7], [8, 9], [10], [11], [12], [13], [14]]
    folds = {groups[j + 1][-1]: j for j in range(len(units) - 1)}
    unit_tiles = []
    for j, unit in enumerate(units):
        tiles = [pieces[m](unit_tiles[folds[m]] if m in folds else None) for m in groups[j]]
        unit_tiles.append(unit(functools.reduce(lambda a, b: a + b, tiles)))
        if j == n_blocks - 1:
            tail = win_scr[tm:tm + CONV_HIST, :]
            win_scr[0:CONV_HIST, :] = tail
            ctail_ref[0] = tail


def _proj(x, g, w_a, w_b, w_alr, w_au, b_al, dww, dwb, lng, lnb, ctail0, tm):
    B, S, D = x.shape
    grid = (B, S // tm)
    row = lambda n: pl.BlockSpec((1, tm, n), lambda b, i: (b, i, 0))
    bf = lambda n: jax.ShapeDtypeStruct((B, S, n), BF16)
    return pl.pallas_call(
        functools.partial(_proj_kernel, tm=tm),
        grid=grid,
        in_specs=[row(D), _const_spec((1, D)), _const_spec((D, N_GROUP[0])), _const_spec((D, N_GROUP[1])),
                  _const_spec((D, RANK_PAD)),
                  _const_spec((RANK_PAD, GLA_DK)), _const_spec((1, GLA_DK)),
                  _const_spec((CONF_K, D)), _const_spec((1, D)), _const_spec((1, D)), _const_spec((1, D)),
                  _const_spec((CONV_HIST, D))],
        out_specs=[row(GLA_DK), row(GLA_DK), row(GLA_DV), row(GLA_DV), row(GLA_DK), row(D), row(D), row(D),
                   pl.BlockSpec((1, CONV_HIST, D), lambda b, i: (b, 0, 0))],
        out_shape=[bf(GLA_DK), bf(GLA_DK), bf(GLA_DV), bf(GLA_DV),
                   jax.ShapeDtypeStruct((B, S, GLA_DK), F32), bf(D), bf(D), bf(D),
                   jax.ShapeDtypeStruct((B, CONV_HIST, D), F32)],
        scratch_shapes=[pltpu.VMEM((CONV_HIST + tm, D), F32),
                        pltpu.VMEM((2, SUBLANES - 1, CONV_HIST + tm, LANES), F32),
                        pltpu.VMEM((tm, D), F32)],
        compiler_params=_params(("parallel", "arbitrary")),
        name="proj",
    )(x, g, w_a, w_b, w_alr, w_au, b_al, dww, dwb, lng, lnb, ctail0)


def _gla_masks():
    i_ = lax.broadcasted_iota(jnp.int32, (CHUNK, CHUNK), 0)
    j_ = lax.broadcasted_iota(jnp.int32, (CHUNK, CHUNK), 1)
    sh = lambda x, n: jnp.right_shift(x, n)
    log2 = lambda n: n.bit_length() - 1
    one = jnp.ones((CHUNK, CHUNK), F32)
    zero = jnp.zeros((CHUNK, CHUNK), F32)
    sel = lambda m: jnp.where(m, one, zero)
    up = lambda x, n: jnp.bitwise_and(sh(x, n), 1) == 1
    pivots, level_masks = [], []
    for lvl in GLA_LEVELS:
        blk = log2(2 * lvl)
        pivots.append(sel(j_ < jnp.left_shift(sh(i_, blk), blk) + lvl))
        level_masks.append(sel(up(i_, log2(lvl))) * sel(~up(j_, log2(lvl))) * sel(sh(i_, blk) == sh(j_, blk)) > 0.5)
    cum_mat = jnp.concatenate([sel(j_ <= i_)] + pivots + [one], axis=0).astype(BF16)
    md = sel(sh(i_, log2(SUB)) == sh(j_, log2(SUB))) * sel(j_ <= i_) > 0.5
    return cum_mat, (level_masks, md)


def _nt_dot(a, b):
    return lax.dot_general(a, b, (((1,), (1,)), ((), ())), preferred_element_type=F32)


def _gla_head(qf, kf, vb, la, st, cum_mat, masks, nch, row_scr):
    level_masks, md = masks
    n_rows = len(GLA_LEVELS) + 2
    la = la * LOG2_E
    la_hi = la.astype(BF16)
    la_lo = (la - la_hi.astype(F32)).astype(BF16)
    la2 = jnp.concatenate([la_hi, la_lo], axis=1)
    parts = [[] for _ in range(n_rows)]
    for c in range(nch):
        cs = jnp.dot(cum_mat, la2[c * CHUNK:(c + 1) * CHUNK], preferred_element_type=F32)
        cs = cs[:, :GLA_DKH] + cs[:, GLA_DKH:]
        for n in range(n_rows):
            parts[n].append(cs[n * CHUNK:(n + 1) * CHUNK])
    b, *pivots, bl = (jnp.concatenate(p, axis=0) for p in parts)

    q_in = (qf * jnp.exp2(b)).astype(BF16)
    k_st = (kf * jnp.exp2(bl - b)).astype(BF16)
    q_lvl = [(qf * jnp.exp2(b - p)).astype(BF16) for p in pivots]
    k_lvl = [(kf * jnp.exp2(p - b)).astype(BF16) for p in pivots]
    col = lax.broadcasted_iota(jnp.int32, (SUB, CHUNK), 1)
    row_scr[0] = kf
    row_scr[1] = b

    outs = []
    for c in range(nch):
        rs_ = slice(c * CHUNK, (c + 1) * CHUNK)
        diag = []
        for g in range(CHUNK // SUB):
            r0 = c * CHUNK + g * SUB
            qb = qf[r0:r0 + SUB]
            bb = b[r0:r0 + SUB]
            a_g = jnp.zeros((SUB, CHUNK), F32)
            for j in range(SUB):
                k_j = row_scr[0, r0 + j:r0 + j + 1, :]
                b_j = row_scr[1, r0 + j:r0 + j + 1, :]
                p = qb * k_j * jnp.exp2(bb - b_j)
                a_g = jnp.where(col == g * SUB + j, jnp.sum(p, axis=1, keepdims=True), a_g)
            diag.append(a_g)
        a = jnp.where(md, jnp.concatenate(diag, axis=0), 0.0)
        for mask, q_l, k_l in zip(level_masks, q_lvl, k_lvl):
            a = jnp.where(mask, _nt_dot(q_l[rs_], k_l[rs_]), a)
        v_c = vb[rs_]
        o = jnp.dot(a.astype(BF16), v_c, preferred_element_type=F32) + _nt_dot(q_in[rs_], st.astype(BF16))
        outs.append(o)
        upd = lax.dot_general(v_c, k_st[rs_], (((0,), (0,)), ((), ())), preferred_element_type=F32)
        st = st * jnp.exp2(bl[c * CHUNK:c * CHUNK + 1]) + upd
    return jnp.concatenate(outs, axis=0), st


def _mix_kernel(q_ref, k_ref, v_ref, la_ref, rs_ref, cs_ref, sgg_ref, sgc_ref, x_ref,
                gn_ref, s0_ref, wgo_ref, wco_ref, wout_ref, h1_ref, sfin_ref, *scratch, nch):
    st_scrs, row_scrs = scratch[:GLA_HEADS], scratch[GLA_HEADS:]
    t = pl.program_id(1)

    @pl.when(t == 0)
    def _():
        for h in range(GLA_HEADS):
            st_scrs[h][...] = s0_ref[h]

    cum_mat, masks = _gla_masks()
    merged = sgc_ref[0].astype(F32) * jnp.dot(cs_ref[0], wco_ref[...], preferred_element_type=F32)
    parts = []
    for h in range(GLA_HEADS):
        kc = slice(h * GLA_DKH, (h + 1) * GLA_DKH)
        vc = slice(h * GLA_DVH, (h + 1) * GLA_DVH)
        o, st = _gla_head(q_ref[0, :, kc].astype(F32), k_ref[0, :, kc].astype(F32), v_ref[0, :, vc],
                          la_ref[0, :, kc], st_scrs[h][...], cum_mat, masks, nch, row_scrs[h])
        st_scrs[h][...] = st
        ms = jnp.mean(o * o, axis=-1, keepdims=True)
        on = o * lax.rsqrt(ms + RMS_EPS) * gn_ref[h]
        og = (on * rs_ref[0, :, vc].astype(F32)).astype(BF16)
        parts.append(jnp.dot(og, wgo_ref[vc, :], preferred_element_type=F32))

    br_gla = (parts[0] + parts[1]) + (parts[2] + parts[3])
    merged = (merged + sgg_ref[0].astype(F32) * br_gla).astype(BF16)
    h1_ref[0] = x_ref[0] + jnp.dot(merged, wout_ref[...], preferred_element_type=F32)

    @pl.when(t == pl.num_programs(1) - 1)
    def _():
        for h in range(GLA_HEADS):
            sfin_ref[0, h] = st_scrs[h][...]


def _mix(q, k, v, la, rs, cs, sgg, sgc, x, gn, s0, wgo, wco, wout, tb):
    B, S, D = x.shape
    grid = (B, S // tb)
    row = lambda n: pl.BlockSpec((1, tb, n), lambda b, t: (b, t, 0))
    st_shape = (GLA_HEADS, GLA_DVH, GLA_DKH)
    return pl.pallas_call(
        functools.partial(_mix_kernel, nch=tb // CHUNK),
        grid=grid,
        in_specs=[row(GLA_DK), row(GLA_DK), row(GLA_DV), row(GLA_DK), row(GLA_DV), row(D), row(D), row(D), row(D),
                  _const_spec((GLA_HEADS, 1, GLA_DVH)), _const_spec(st_shape),
                  _const_spec((GLA_DV, D)), _const_spec((D, D)), _const_spec((D, D))],
        out_specs=[row(D), pl.BlockSpec((1,) + st_shape, lambda b, t: (b, 0, 0, 0))],
        out_shape=[jax.ShapeDtypeStruct((B, S, D), F32), jax.ShapeDtypeStruct((B,) + st_shape, F32)],
        scratch_shapes=([pltpu.VMEM(st_shape[1:], F32)] * GLA_HEADS
                        + [pltpu.VMEM((2, tb, GLA_DKH), F32)] * GLA_HEADS),
        compiler_params=_params(("parallel", "arbitrary")),
        name="mix",
    )(q, k, v, la, rs, cs, sgg, sgc, x, gn, s0, wgo, wco, wout)


def _ffn_kernel(h1_ref, g_ref, wup_ref, dww_ref, dwb_ref, wdn_ref, gfin_ref, atail0_ref,
                out_ref, atail_ref, win_scr, *, tm):
    i = pl.program_id(1)

    @pl.when(i == 0)
    def _():
        win_scr[0:FFN_HIST, :] = atail0_ref[...]

    h1 = h1_ref[0]
    ms = jnp.mean(h1 * h1, axis=-1, keepdims=True)
    u = (h1 * lax.rsqrt(ms + RMS_EPS) * g_ref[...]).astype(BF16)
    a = jnp.dot(u, wup_ref[:, 0:D_FF], preferred_element_type=F32)
    bv = jnp.dot(u, wup_ref[:, D_FF:2 * D_FF], preferred_element_type=F32)
    win_scr[FFN_HIST:FFN_HIST + tm, :] = a
    off0 = FFN_HIST - (FFN_K - 1)
    ac = dwb_ref[...] + dww_ref[0:1, :] * win_scr[off0:off0 + tm, :]
    for j in range(1, FFN_K):
        ac = ac + dww_ref[j:j + 1, :] * win_scr[off0 + j:off0 + j + tm, :]
    hdn = (ac * _sigmoid(ac) * bv).astype(BF16)
    h2 = h1 + jnp.dot(hdn, wdn_ref[...], preferred_element_type=F32)
    ms2 = jnp.mean(h2 * h2, axis=-1, keepdims=True)
    out_ref[0] = h2 * lax.rsqrt(ms2 + RMS_EPS) * gfin_ref[...]
    tail = win_scr[tm:tm + FFN_HIST, :]
    win_scr[0:FFN_HIST, :] = tail
    atail_ref[0] = tail


def _ffn(h1, g, wup, dww, dwb, wdn, gfin, atail0, tm):
    B, S, D = h1.shape
    grid = (B, S // tm)
    row = pl.BlockSpec((1, tm, D), lambda b, i: (b, i, 0))
    return pl.pallas_call(
        functools.partial(_ffn_kernel, tm=tm),
        grid=grid,
        in_specs=[row, _const_spec((1, D)), _const_spec((D, 2 * D_FF)), _const_spec((FFN_K, D_FF)),
                  _const_spec((1, D_FF)), _const_spec((D_FF, D)), _const_spec((1, D)),
                  _const_spec((FFN_HIST, D_FF))],
        out_specs=[row, pl.BlockSpec((1, FFN_HIST, D_FF), lambda b, i: (b, 0, 0))],
        out_shape=[jax.ShapeDtypeStruct((B, S, D), F32), jax.ShapeDtypeStruct((B, FFN_HIST, D_FF), F32)],
        scratch_shapes=[pltpu.VMEM((FFN_HIST + tm, D_FF), F32)],
        compiler_params=_params(("parallel", "arbitrary")),
        name="ffn",
    )(h1, g, wup, dww, dwb, wdn, gfin, atail0)


def _block(h, carries, wts, tiles, la_mask_rows=0):
    s0, ctail0, atail0 = carries
    tm_proj, tb_mix, tm_ffn = tiles
    q, k, v, rs, la, cs, sgg, sgc, ctail = _proj(
        h, wts["g_mix"], wts["w_a"], wts["w_b"], wts["w_alr"], wts["w_au"], wts["b_al"],
        wts["dw_w"], wts["dw_b"], wts["ln_g"], wts["ln_b"], ctail0, tm_proj)
    if la_mask_rows:
        la = la.at[:, :la_mask_rows].set(0.0)
    h1, sfin = _mix(q, k, v, la, rs, cs, sgg, sgc, h, wts["g_gla"], s0,
                    wts["w_gla_o"], wts["w_conf_o"], wts["w_out"], tb_mix)
    out, atail = _ffn(h1, wts["g_ffn"], wts["w_up"], wts["ffn_dw_w"], wts["ffn_dw_b"], wts["w_down"],
                      wts["g_fin"], atail0, tm_ffn)
    return out, (sfin[0], ctail[0], atail[0])


def kernel(x, meta_tokens, norm_mix_g, w_in, w_alpha_up, b_alpha, gla_norm_g, w_gla_o, conf_dw_w, conf_dw_b,
           conf_ln_g, conf_ln_b, w_conf_o, w_out, norm_ffn_g, w_up, ffn_dw_w, ffn_dw_b, w_down, final_norm_g):
    assert norm_mix_g.shape[0] == 1, "single-layer block"
    B, S, D = x.shape
    lr0 = 2 * GLA_DK + 2 * GLA_DV
    w_in0 = w_in.astype(BF16)[0]
    wts = dict(
        g_mix=norm_mix_g[0][None],
        w_a=w_in0[:, :lr0], w_b=w_in0[:, lr0 + GLA_RANK:],
        w_alr=jnp.pad(w_in0[:, lr0:lr0 + GLA_RANK], ((0, 0), (0, RANK_PAD - GLA_RANK))),
        w_au=jnp.pad(w_alpha_up[0], ((0, RANK_PAD - GLA_RANK), (0, 0))).astype(BF16),
        b_al=b_alpha[0][None],
        g_gla=gla_norm_g[0].reshape(GLA_HEADS, 1, GLA_DVH),
        dw_w=conf_dw_w[0], dw_b=conf_dw_b[0][None], ln_g=conf_ln_g[0][None], ln_b=conf_ln_b[0][None],
        w_gla_o=w_gla_o[0].astype(BF16), w_conf_o=w_conf_o[0].astype(BF16), w_out=w_out[0].astype(BF16),
        g_ffn=norm_ffn_g[0][None], w_up=w_up[0].astype(BF16),
        ffn_dw_w=ffn_dw_w[0], ffn_dw_b=ffn_dw_b[0][None], w_down=w_down[0].astype(BF16),
        g_fin=final_norm_g[None],
    )
    zero_carries = (jnp.zeros((GLA_HEADS, GLA_DVH, GLA_DKH), F32),
                    jnp.zeros((CONV_HIST, D), F32),
                    jnp.zeros((FFN_HIST, D_FF), F32))
    h_meta = jnp.concatenate([jnp.zeros((META_PAD, D), x.dtype), meta_tokens.astype(x.dtype)], axis=0)[None]
    _, carries = _block(h_meta, zero_carries, wts, (CHUNK, CHUNK, CHUNK), la_mask_rows=META_PAD)
    out, _ = _block(x, carries, wts, (SEQ_TILE,) * 3)
    return out
```
